```python
import math
import jax, jax.numpy as jnp
from jax import lax
import numpy as np

D_MODEL = 2048
BATCH = 8
SEQ = 2048
DEPTH = 2

N_A_LAYERS = DEPTH // 2
N_B_LAYERS = DEPTH - N_A_LAYERS

CONV_WIDTH = 31

HEAD_DIM = 128
N_HEADS = D_MODEL // HEAD_DIM
N_KV_HEADS = 4
GROUP = N_HEADS // N_KV_HEADS
WINDOWS = (128, 512, 2048)
DILATIONS = (1, 4, 16)
BLOCK = 128
PAD_UNIT = math.lcm(*DILATIONS) * BLOCK

ROPE_THETA = 500000.0
ROT_DIM = HEAD_DIM // 4

D_FF = 4 * D_MODEL

NORM_EPS = 1e-6
LN_EPS = 1e-5

kernel_name = "yoco_conformer_dilated_hybrid"


def rmsnorm(x, g):
    x32 = x.astype(jnp.float32)
    y = x32 * lax.rsqrt(jnp.mean(x32 * x32, axis=-1, keepdims=True) + NORM_EPS)
    return (y * g.astype(jnp.float32)).astype(x.dtype)


def layernorm(x, g, b):
    x32 = x.astype(jnp.float32)
    mu = jnp.mean(x32, axis=-1, keepdims=True)
    xc = x32 - mu
    var = jnp.mean(xc * xc, axis=-1, keepdims=True)
    y = xc * lax.rsqrt(var + LN_EPS) * g.astype(jnp.float32) + b.astype(jnp.float32)
    return y.astype(x.dtype)


def rope_tables(seq):
    pos = jnp.arange(seq, dtype=jnp.float32)
    inv = ROPE_THETA ** (-jnp.arange(0, ROT_DIM, 2, dtype=jnp.float32) / ROT_DIM)
    ang = pos[:, None] * inv[None, :]
    return jnp.cos(ang), jnp.sin(ang)


def partial_rope(x, cos, sin):
    half = ROT_DIM // 2
    shape = (1, x.shape[1]) + (1,) * (x.ndim - 3) + (half,)
    c = cos.reshape(shape).astype(x.dtype)
    s = sin.reshape(shape).astype(x.dtype)
    x1 = x[..., :half]
    x2 = x[..., half:ROT_DIM]
    return jnp.concatenate([x1 * c - x2 * s, x2 * c + x1 * s, x[..., ROT_DIM:]], axis=-1)


def conformer_conv(y, w_in, b_in, w_dw, b_dw, ln_g, ln_b, w_out, b_out):
    u = y @ w_in + b_in
    a, gate = jnp.split(u, 2, axis=-1)
    u = a * jax.nn.sigmoid(gate)
    u = lax.conv_general_dilated(
        u, w_dw[:, None, :], window_strides=(1,), padding=[(CONV_WIDTH - 1, 0)],
        dimension_numbers=("NWC", "WIO", "NWC"), feature_group_count=u.shape[-1]) + b_dw
    u = jax.nn.silu(layernorm(u, ln_g, ln_b))
    return u @ w_out + b_out


def sq_relu_mlp(y, w_in, w_out):
    h = jax.nn.relu(y @ w_in)
    return (h * h) @ w_out


def shared_kv(h, g, w_kv, cos, sin):
    B, S, _ = h.shape
    kv = rmsnorm(h, g) @ w_kv
    k, v = jnp.split(kv, 2, axis=-1)
    k = partial_rope(k.reshape(B, S, N_KV_HEADS, HEAD_DIM), cos, sin)
    v = v.reshape(B, S, N_KV_HEADS, HEAD_DIM)
    return k, v


def _with_prev_block(t):
    prev = jnp.pad(t[:, :-1], ((0, 0), (1, 0)) + ((0, 0),) * (t.ndim - 2))
    return jnp.concatenate([prev, t], axis=2)


def dilated_branch(q, k, v, dil, steps):
    B, Sp, KVH, G, hd = q.shape
    n = Sp // dil // BLOCK
    qb = q.reshape(B, n, BLOCK, dil, KVH, G, hd)
    kw = _with_prev_block(k.reshape(B, n, BLOCK, dil, KVH, hd))
    vw = _with_prev_block(v.reshape(B, n, BLOCK, dil, KVH, hd))
    s = jnp.einsum("bncrhgd,bnkrhd->bnrhgck", qb, kw,
                   preferred_element_type=jnp.float32) * (1.0 / math.sqrt(hd))
    c = jnp.arange(BLOCK)[:, None]
    j = jnp.arange(2 * BLOCK)[None, :]
    dist = c + BLOCK - j
    band = (dist >= 0) & (dist <= steps)
    first = (jnp.arange(n) == 0)[:, None, None]
    valid = band[None] & ~(first & (j < BLOCK)[None])
    s = jnp.where(valid[None, :, None, None, None], s, -jnp.inf)
    m = jnp.max(s, axis=-1, keepdims=True)
    p = jnp.exp(s - m)
    l = jnp.sum(p, axis=-1, keepdims=True)
    o = jnp.einsum("bnrhgck,bnkrhd->bncrhgd", p.astype(v.dtype), vw)
    l_t = jnp.transpose(l[..., 0], (0, 1, 5, 2, 3, 4))
    lse = jnp.transpose((m + jnp.log(l))[..., 0], (0, 1, 5, 2, 3, 4))
    o = o.astype(jnp.float32) / l_t[..., None]
    return o.reshape(B, Sp, KVH, G, hd), lse.reshape(B, Sp, KVH, G)


def dilated_mixture(q, k, v):
    S = q.shape[1]
    Sp = -(-S // PAD_UNIT) * PAD_UNIT
    pad = Sp - S
    q = jnp.pad(q, ((0, 0), (0, pad), (0, 0), (0, 0), (0, 0)))
    k = jnp.pad(k, ((0, 0), (0, pad), (0, 0), (0, 0)))
    v = jnp.pad(v, ((0, 0), (0, pad), (0, 0), (0, 0)))
    outs, lses = [], []
    for win, dil in zip(WINDOWS, DILATIONS):
        o, lse = dilated_branch(q, k, v, dil, win // dil)
        outs.append(o)
        lses.append(lse)
    w = jax.nn.softmax(jnp.stack(lses, axis=0), axis=0)
    o = jnp.einsum("ibshg,ibshgd->bshgd", w, jnp.stack(outs, axis=0))
    return o[:, :S].astype(q.dtype)


def dilated_attn_layer(y, k, v, w_q, w_o, cos, sin):
    B, S, _ = y.shape
    q = partial_rope((y @ w_q).reshape(B, S, N_KV_HEADS, GROUP, HEAD_DIM), cos, sin)
    o = dilated_mixture(q, k, v)
    return o.reshape(B, S, N_HEADS * HEAD_DIM) @ w_o


def setup_inputs(seed: int = 0) -> dict:
    key = jax.random.key(seed)
    ks = jax.random.split(key, 20)
    D = D_MODEL
    f32 = jnp.float32

    def nrm(k, shape, scale):
        return jax.random.normal(k, shape, f32) * scale

    return {
        "x": nrm(ks[0], (BATCH, SEQ, D), 1.0),
        "norm_mix": 1.0 + nrm(ks[1], (DEPTH, D), 0.01),
        "norm_mlp": 1.0 + nrm(ks[2], (DEPTH, D), 0.01),
        "conv_w_in": nrm(ks[3], (N_A_LAYERS, D, 2 * D), D ** -0.5),
        "conv_b_in": nrm(ks[4], (N_A_LAYERS, 2 * D), 0.02),
        "conv_w_dw": nrm(ks[5], (N_A_LAYERS, CONV_WIDTH, D), CONV_WIDTH ** -0.5),
        "conv_b_dw": nrm(ks[6], (N_A_LAYERS, D), 0.02),
        "conv_ln_g": 1.0 + nrm(ks[7], (N_A_LAYERS, D), 0.01),
        "conv_ln_b": nrm(ks[8], (N_A_LAYERS, D), 0.02),
        "conv_w_out": nrm(ks[9], (N_A_LAYERS, D, D), D ** -0.5),
        "conv_b_out": nrm(ks[10], (N_A_LAYERS, D), 0.02),
        "kv_norm": 1.0 + nrm(ks[11], (D,), 0.01),
        "w_kv": nrm(ks[12], (D, 2 * N_KV_HEADS * HEAD_DIM), D ** -0.5),
        "attn_w_q": nrm(ks[13], (N_B_LAYERS, D, N_HEADS * HEAD_DIM), D ** -0.5),
        "attn_w_o": nrm(ks[14], (N_B_LAYERS, N_HEADS * HEAD_DIM, D), (N_HEADS * HEAD_DIM) ** -0.5),
        "mlp_w_in": nrm(ks[15], (DEPTH, D, D_FF), D ** -0.5),
        "mlp_w_out": nrm(ks[16], (DEPTH, D_FF, D), D_FF ** -0.5),
        "final_norm": 1.0 + nrm(ks[17], (D,), 0.01),
    }


def reference(x, norm_mix, norm_mlp, conv_w_in, conv_b_in, conv_w_dw, conv_b_dw,
              conv_ln_g, conv_ln_b, conv_w_out, conv_b_out, kv_norm, w_kv,
              attn_w_q, attn_w_o, mlp_w_in, mlp_w_out, final_norm):
    S = x.shape[1]
    cos, sin = rope_tables(S)
    h = x
    k_sh = None
    v_sh = None
    for layer in range(DEPTH):
        y = rmsnorm(h, norm_mix[layer])
        if layer < N_A_LAYERS:
            a = layer
            h = h + conformer_conv(y, conv_w_in[a], conv_b_in[a], conv_w_dw[a], conv_b_dw[a],
                                   conv_ln_g[a], conv_ln_b[a], conv_w_out[a], conv_b_out[a])
        else:
            j = layer - N_A_LAYERS
            h = h + dilated_attn_layer(y, k_sh, v_sh, attn_w_q[j], attn_w_o[j], cos, sin)
        h = h + sq_relu_mlp(rmsnorm(h, norm_mlp[layer]), mlp_w_in[layer], mlp_w_out[layer])
        if layer == N_A_LAYERS - 1:
            k_sh, v_sh = shared_kv(h, kv_norm, w_kv, cos, sin)
    return rmsnorm(h, final_norm)
```

```python
import functools
import math

import jax
import jax.numpy as jnp
from jax import lax
from jax.experimental import pallas as pl
from jax.experimental.pallas import tpu as pltpu

F32 = jnp.float32
BF16 = jnp.bfloat16

D_MODEL = 2048
CONV_WIDTH = 31
HEAD_DIM = 128
N_HEADS = 16
N_KV_HEADS = 4
GROUP = N_HEADS // N_KV_HEADS
KV_DIM = N_KV_HEADS * HEAD_DIM
WINDOWS = (128, 512, 2048)
DILATIONS = (1, 4, 16)
BLOCK = 128
ROPE_THETA = 500000.0
ROT_DIM = HEAD_DIM // 4
D_FF = 4 * D_MODEL
NORM_EPS = 1e-6
LN_EPS = 1e-5

LANES = 128
HALO = 32
VMEM_LIMIT = 56 * 1024 * 1024
MASK_VALUE = -1e30


def _params(sem):
    return pltpu.CompilerParams(dimension_semantics=sem, vmem_limit_bytes=VMEM_LIMIT)


def _rmsnorm_bf16(x, g):
    ms = jnp.mean(x * x, axis=-1, keepdims=True)
    return (x * lax.rsqrt(ms + NORM_EPS) * g).astype(BF16)


def _glu_kernel(x_ref, g_ref, wa_ref, wg_ref, ba_ref, bg_ref, o_ref, xn_ref):
    @pl.when(pl.program_id(1) == 0)
    def _():
        xn_ref[...] = _rmsnorm_bf16(x_ref[...], g_ref[...])

    xn = xn_ref[...]
    a = jnp.dot(xn, wa_ref[...], preferred_element_type=F32) + ba_ref[...]
    gate = jnp.dot(xn, wg_ref[...], preferred_element_type=F32) + bg_ref[...]
    o_ref[...] = a * jax.nn.sigmoid(gate)


def _glu(x, g, w, b, *, tm=512, tn=512):
    T, D = x.shape
    nj = D // tn
    return pl.pallas_call(
        _glu_kernel,
        grid=(T // tm, nj),
        in_specs=[
            pl.BlockSpec((tm, D), lambda i, j: (i, 0)),
            pl.BlockSpec((1, D), lambda i, j: (0, 0)),
            pl.BlockSpec((D, tn), lambda i, j: (0, j)),
            pl.BlockSpec((D, tn), lambda i, j: (0, j + nj)),
            pl.BlockSpec((1, tn), lambda i, j: (0, j)),
            pl.BlockSpec((1, tn), lambda i, j: (0, j + nj)),
        ],
        out_specs=pl.BlockSpec((tm, tn), lambda i, j: (i, j)),
        out_shape=jax.ShapeDtypeStruct((T, D), F32),
        scratch_shapes=[pltpu.VMEM((tm, D), BF16)],
        compiler_params=_params(("parallel", "arbitrary")),
        name="glu",
    )(x, g, w, w, b, b)


def _conv_out_kernel(halo_ref, u_ref, wdw_ref, bdw_ref, lng_ref, lnb_ref, wo_ref, bo_ref,
                     h_ref, o_ref, ext_ref, cv_ref, *, ts):
    i = pl.program_id(1)

    @pl.when(i == 0)
    def _():
        ext_ref[0:HALO, :] = jnp.zeros((HALO, D_MODEL), F32)

    @pl.when(i > 0)
    def _():
        ext_ref[0:HALO, :] = halo_ref[0]

    ext_ref[HALO:HALO + ts, :] = u_ref[0]

    base = HALO - (CONV_WIDTH - 1)

    def chan_body(c, carry):
        cs = pl.ds(pl.multiple_of(c * LANES, LANES), LANES)
        acc = jnp.zeros((ts, LANES), F32) + bdw_ref[:, cs]
        for k in range(CONV_WIDTH):
            acc = acc + ext_ref[pl.ds(base + k, ts), cs] * wdw_ref[k:k + 1, cs]
        cv_ref[:, cs] = acc
        return carry

    lax.fori_loop(0, D_MODEL // LANES, chan_body, 0)

    cv = cv_ref[...]
    mu = jnp.mean(cv, axis=-1, keepdims=True)
    xc = cv - mu
    var = jnp.mean(xc * xc, axis=-1, keepdims=True)
    y = xc * lax.rsqrt(var + LN_EPS) * lng_ref[...] + lnb_ref[...]
    y = (y * jax.nn.sigmoid(y)).astype(BF16)
    o_ref[...] = (h_ref[...] + bo_ref[...]
                  + jnp.dot(y, wo_ref[...], preferred_element_type=F32))


def _conv_out(u, h, w_dw, b_dw, ln_g, ln_b, w_out, b_out, *, batch, seq, ts=256):
    D = D_MODEL
    n = seq // ts
    u3 = u.reshape(batch, seq, D)
    hpb = ts // HALO
    row = lambda b, i: (b * n + i, 0)
    const = lambda b, i: (0, 0)
    return pl.pallas_call(
        functools.partial(_conv_out_kernel, ts=ts),
        grid=(batch, n),
        in_specs=[
            pl.BlockSpec((1, HALO, D), lambda b, i: (b, jnp.maximum(i * hpb - 1, 0), 0)),
            pl.BlockSpec((1, ts, D), lambda b, i: (b, i, 0)),
            pl.BlockSpec((CONV_WIDTH, D), const),
            pl.BlockSpec((1, D), const),
            pl.BlockSpec((1, D), const),
            pl.BlockSpec((1, D), const),
            pl.BlockSpec((D, D), const),
            pl.BlockSpec((1, D), const),
            pl.BlockSpec((ts, D), row),
        ],
        out_specs=pl.BlockSpec((ts, D), row),
        out_shape=jax.ShapeDtypeStruct((batch * seq, D), F32),
        scratch_shapes=[pltpu.VMEM((HALO + ts, D), F32), pltpu.VMEM((ts, D), F32)],
        compiler_params=_params(("parallel", "arbitrary")),
        name="conv_out",
    )(u3, u3, w_dw, b_dw, ln_g, ln_b, w_out, b_out, h)


def _mlp_kernel(h_ref, g_ref, w1_ref, w2_ref, gf_ref, o_ref, xn_ref, *, final_norm):
    f = pl.program_id(1)

    @pl.when(f == 0)
    def _():
        h = h_ref[...]
        xn_ref[...] = _rmsnorm_bf16(h, g_ref[...])
        o_ref[...] = h

    a = jnp.maximum(jnp.dot(xn_ref[...], w1_ref[...], preferred_element_type=F32), 0.0)
    o_ref[...] += jnp.dot((a * a).astype(BF16), w2_ref[...], preferred_element_type=F32)

    if final_norm:
        @pl.when(f == pl.num_programs(1) - 1)
        def _():
            h = o_ref[...]
            ms = jnp.mean(h * h, axis=-1, keepdims=True)
            o_ref[...] = h * lax.rsqrt(ms + NORM_EPS) * gf_ref[...]


def _mlp(h, g, w1, w2, gf, *, final_norm, tm=512, tf=1024):
    T, D = h.shape
    return pl.pallas_call(
        functools.partial(_mlp_kernel, final_norm=final_norm),
        grid=(T // tm, D_FF // tf),
        in_specs=[
            pl.BlockSpec((tm, D), lambda i, f: (i, 0)),
            pl.BlockSpec((1, D), lambda i, f: (0, 0)),
            pl.BlockSpec((D, tf), lambda i, f: (0, f)),
            pl.BlockSpec((tf, D), lambda i, f: (f, 0)),
            pl.BlockSpec((1, D), lambda i, f: (0, 0)),
        ],
        out_specs=pl.BlockSpec((tm, D), lambda i, f: (i, 0)),
        out_shape=jax.ShapeDtypeStruct((T, D), F32),
        scratch_shapes=[pltpu.VMEM((tm, D), BF16)],
        compiler_params=_params(("parallel", "arbitrary")),
        name="mlp",
    )(h, g, w1, w2, gf)


def _rope_tile(x, cos_t, sin_t):
    lane = lax.broadcasted_iota(jnp.int32, (x.shape[0], HEAD_DIM), 1)
    first_half = lane < ROT_DIM // 2
    outs = []
    for hh in range(x.shape[1] // HEAD_DIM):
        xh = x[:, hh * HEAD_DIM:(hh + 1) * HEAD_DIM]
        partner = jnp.where(first_half,
                            pltpu.roll(xh, HEAD_DIM - ROT_DIM // 2, 1),
                            pltpu.roll(xh, ROT_DIM // 2, 1))
        outs.append(xh * cos_t + partner * sin_t)
    return jnp.concatenate(outs, axis=1)


def _proj_kernel(x_ref, g_ref, w_ref, cos_ref, sin_ref, o_ref, xn_ref, *, n_rope, scale):
    j = pl.program_id(1)

    @pl.when(j == 0)
    def _():
        xn_ref[...] = _rmsnorm_bf16(x_ref[...], g_ref[...])

    acc = jnp.dot(xn_ref[...], w_ref[...], preferred_element_type=F32)

    @pl.when(j < n_rope)
    def _():
        o_ref[...] = (_rope_tile(acc, cos_ref[...], sin_ref[...]) * scale).astype(o_ref.dtype)

    @pl.when(j >= n_rope)
    def _():
        o_ref[...] = acc.astype(o_ref.dtype)


def _proj(x, g, w, cos_t, sin_t, *, seq, n_rope, scale, tm=512, tn=512):
    T, D = x.shape
    N = w.shape[1]
    spb = seq // tm
    return pl.pallas_call(
        functools.partial(_proj_kernel, n_rope=n_rope, scale=scale),
        grid=(T // tm, N // tn),
        in_specs=[
            pl.BlockSpec((tm, D), lambda i, j: (i, 0)),
            pl.BlockSpec((1, D), lambda i, j: (0, 0)),
            pl.BlockSpec((D, tn), lambda i, j: (0, j)),
            pl.BlockSpec((tm, HEAD_DIM), lambda i, j: (i % spb, 0)),
            pl.BlockSpec((tm, HEAD_DIM), lambda i, j: (i % spb, 0)),
        ],
        out_specs=pl.BlockSpec((tm, tn), lambda i, j: (i, j)),
        out_shape=jax.ShapeDtypeStruct((T, N), BF16),
        scratch_shapes=[pltpu.VMEM((tm, D), BF16)],
        compiler_params=_params(("parallel", "arbitrary")),
        name="proj",
    )(x, g, w, cos_t, sin_t)


def _attn_kernel(q_ref, kvp_ref, kvc_ref, o_ref, st_ref, *, steps):
    n = pl.program_id(2)
    c = lax.broadcasted_iota(jnp.int32, (GROUP * BLOCK, 2 * BLOCK), 0) & (BLOCK - 1)
    j = lax.broadcasted_iota(jnp.int32, (GROUP * BLOCK, 2 * BLOCK), 1)
    lo = c + (BLOCK - steps)
    j_min = jnp.where(n > 0, lo, jnp.maximum(lo, BLOCK))
    valid = jnp.logical_and(j >= j_min, j <= c + BLOCK)
    lane = lax.broadcasted_iota(jnp.int32, (BLOCK, LANES), 1)
    stats = jnp.zeros((BLOCK, LANES), F32)

    for h in range(N_KV_HEADS):
        ks = slice(h * HEAD_DIM, (h + 1) * HEAD_DIM)
        vs = slice(KV_DIM + h * HEAD_DIM, KV_DIM + (h + 1) * HEAD_DIM)
        kk = jnp.concatenate([kvp_ref[0, :, ks], kvc_ref[0, :, ks]], axis=0)
        vv = jnp.concatenate([kvp_ref[0, :, vs], kvc_ref[0, :, vs]], axis=0)
        qh = jnp.concatenate(
            [q_ref[0, :, (h * GROUP + g) * HEAD_DIM:(h * GROUP + g + 1) * HEAD_DIM]
             for g in range(GROUP)], axis=0)
        s = lax.dot_general(qh, kk, (((1,), (1,)), ((), ())), preferred_element_type=F32)
        s = jnp.where(valid, s, MASK_VALUE)
        m = jnp.max(s, axis=-1, keepdims=True)
        p = jnp.exp(s - m)
        l = jnp.sum(p, axis=-1, keepdims=True)
        o = jnp.dot(p.astype(BF16), vv, preferred_element_type=F32) / l
        lse = m + jnp.log(l)
        for g in range(GROUP):
            head = h * GROUP + g
            rows = slice(g * BLOCK, (g + 1) * BLOCK)
            o_ref[0, :, head * HEAD_DIM:(head + 1) * HEAD_DIM] = o[rows].astype(o_ref.dtype)
            stats = jnp.where(lane == head, lse[rows], stats)

    st_ref[0] = stats


def _attn_branch(q, kv, *, batch, seq, dil, steps):
    sd = seq // dil
    nblk = sd // BLOCK
    qd = N_HEADS * HEAD_DIM
    qv = q.reshape(batch, sd, dil * qd)
    kvv = kv.reshape(batch, sd, dil * 2 * KV_DIM)
    o, st = pl.pallas_call(
        functools.partial(_attn_kernel, steps=steps),
        grid=(batch, dil, nblk),
        in_specs=[
            pl.BlockSpec((1, BLOCK, qd), lambda b, r, n: (b, n, r)),
            pl.BlockSpec((1, BLOCK, 2 * KV_DIM), lambda b, r, n: (b, jnp.maximum(n - 1, 0), r)),
            pl.BlockSpec((1, BLOCK, 2 * KV_DIM), lambda b, r, n: (b, n, r)),
        ],
        out_specs=[
            pl.BlockSpec((1, BLOCK, qd), lambda b, r, n: (b, n, r)),
            pl.BlockSpec((1, BLOCK, LANES), lambda b, r, n: (b, n, r)),
        ],
        out_shape=[
            jax.ShapeDtypeStruct((batch, sd, dil * qd), BF16),
            jax.ShapeDtypeStruct((batch, sd, dil * LANES), F32),
        ],
        compiler_params=_params(("parallel", "parallel", "arbitrary")),
        name=f"attn_d{dil}",
    )(qv, kvv, kvv)
    return o.reshape(batch * seq, qd), st.reshape(batch * seq, LANES)


def _out_proj_kernel(o1_ref, o2_ref, o3_ref, s1_ref, s2_ref, s3_ref, w_ref, h_ref, out_ref, mix_ref):
    l1, l2, l3 = s1_ref[...], s2_ref[...], s3_ref[...]
    mx = jnp.maximum(jnp.maximum(l1, l2), l3)
    e1, e2, e3 = jnp.exp(l1 - mx), jnp.exp(l2 - mx), jnp.exp(l3 - mx)
    inv = 1.0 / (e1 + e2 + e3)
    w1, w2, w3 = e1 * inv, e2 * inv, e3 * inv
    for head in range(N_HEADS):
        cs = slice(head * HEAD_DIM, (head + 1) * HEAD_DIM)
        hs = slice(head, head + 1)
        mix = (w1[:, hs] * o1_ref[:, cs].astype(F32)
               + w2[:, hs] * o2_ref[:, cs].astype(F32)
               + w3[:, hs] * o3_ref[:, cs].astype(F32))
        mix_ref[:, cs] = mix.astype(BF16)
    out_ref[...] = h_ref[...] + jnp.dot(mix_ref[...], w_ref[...], preferred_element_type=F32)


def _out_proj(os_, sts, w, h, *, tm=256):
    T, D = h.shape
    qd = N_HEADS * HEAD_DIM
    row = lambda i: (i, 0)
    const = lambda i: (0, 0)
    return pl.pallas_call(
        _out_proj_kernel,
        grid=(T // tm,),
        in_specs=[pl.BlockSpec((tm, qd), row)] * 3 + [pl.BlockSpec((tm, LANES), row)] * 3 + [
            pl.BlockSpec((qd, D), const),
            pl.BlockSpec((tm, D), row),
        ],
        out_specs=pl.BlockSpec((tm, D), row),
        out_shape=jax.ShapeDtypeStruct((T, D), F32),
        scratch_shapes=[pltpu.VMEM((tm, qd), BF16)],
        compiler_params=_params(("parallel",)),
        name="out_proj",
    )(*os_, *sts, w, h)


def _rope_tables(seq):
    half = ROT_DIM // 2
    pos = jnp.arange(seq, dtype=F32)
    inv = ROPE_THETA ** (-jnp.arange(0, ROT_DIM, 2, dtype=F32) / ROT_DIM)
    ang = pos[:, None] * inv[None, :]
    cos, sin = jnp.cos(ang), jnp.sin(ang)
    cos_t = jnp.concatenate([cos, cos, jnp.ones((seq, HEAD_DIM - ROT_DIM), F32)], axis=1)
    sin_t = jnp.concatenate([-sin, sin, jnp.zeros((seq, HEAD_DIM - ROT_DIM), F32)], axis=1)
    return cos_t, sin_t


def kernel(x, norm_mix, norm_mlp, conv_w_in, conv_b_in, conv_w_dw, conv_b_dw, conv_ln_g, conv_ln_b,
           conv_w_out, conv_b_out, kv_norm, w_kv, attn_w_q, attn_w_o, mlp_w_in, mlp_w_out, final_norm):
    B, S, D = x.shape
    T = B * S
    row = lambda v: v.reshape(1, -1)
    cos_t, sin_t = _rope_tables(S)

    h = x.reshape(T, D)

    u = _glu(h, row(norm_mix[0]), conv_w_in[0].astype(BF16), row(conv_b_in[0]))
    h = _conv_out(u, h, conv_w_dw[0], row(conv_b_dw[0]), row(conv_ln_g[0]), row(conv_ln_b[0]),
                  conv_w_out[0].astype(BF16), row(conv_b_out[0]), batch=B, seq=S)
    h = _mlp(h, row(norm_mlp[0]), mlp_w_in[0].astype(BF16), mlp_w_out[0].astype(BF16),
             row(final_norm), final_norm=False)

    kv = _proj(h, row(kv_norm), w_kv.astype(BF16), cos_t, sin_t, seq=S,
               n_rope=KV_DIM // 512, scale=1.0)

    q = _proj(h, row(norm_mix[1]), attn_w_q[0].astype(BF16), cos_t, sin_t, seq=S,
              n_rope=(N_HEADS * HEAD_DIM) // 512, scale=1.0 / math.sqrt(HEAD_DIM))
    outs, stats = [], []
    for win, dil in zip(WINDOWS, DILATIONS):
        o, st = _attn_branch(q, kv, batch=B, seq=S, dil=dil, steps=win // dil)
        outs.append(o)
        stats.append(st)
    h = _out_proj(outs, stats, attn_w_o[0].astype(BF16), h)
    h = _mlp(h, row(norm_mlp[1]), mlp_w_in[1].astype(BF16), mlp_w_out[1].astype(BF16),
             row(final_norm), final_norm=True)
    return h.reshape(B, S, D)
```

```python
import functools
import math

import jax
import jax.numpy as jnp
from jax import lax
from jax.experimental import pallas as pl
from jax.experimental.pallas import tpu as pltpu

F32 = jnp.float32
BF16 = jnp.bfloat16

D_MODEL = 2048
CONV_WIDTH = 31
HEAD_DIM = 128
N_HEADS = 16
N_KV_HEADS = 4
GROUP = N_HEADS // N_KV_HEADS
KV_DIM = N_KV_HEADS * HEAD_DIM
WINDOWS = (128, 512, 2048)
DILATIONS = (1, 4, 16)
BLOCK = 128
ROPE_THETA = 500000.0
ROT_DIM = HEAD_DIM // 4
D_FF = 4 * D_MODEL
NORM_EPS = 1e-6
LN_EPS = 1e-5

LANES = 128
HALO = 32
VMEM_LIMIT = 56 * 1024 * 1024
MASK_VALUE = -1e30
LOG2_E = math.log2(math.e)


def _params(sem):
    return pltpu.CompilerParams(dimension_semantics=sem, vmem_limit_bytes=VMEM_LIMIT)


def _rmsnorm_bf16(x, g):
    ms = jnp.mean(x * x, axis=-1, keepdims=True)
    return (x * lax.rsqrt(ms + NORM_EPS) * g).astype(BF16)


def _glu_kernel(x_ref, g_ref, wa_ref, wg_ref, ba_ref, bg_ref, o_ref, xn_ref):
    @pl.when(pl.program_id(1) == 0)
    def _():
        xn_ref[...] = _rmsnorm_bf16(x_ref[...], g_ref[...])

    xn = xn_ref[...]
    a = jnp.dot(xn, wa_ref[...], preferred_element_type=F32) + ba_ref[...]
    gate = jnp.dot(xn, wg_ref[...], preferred_element_type=F32) + bg_ref[...]
    o_ref[...] = a * jax.nn.sigmoid(gate)


def _glu(x, g, w, b, *, tm=512, tn=512):
    T, D = x.shape
    nj = D // tn
    return pl.pallas_call(
        _glu_kernel,
        grid=(T // tm, nj),
        in_specs=[
            pl.BlockSpec((tm, D), lambda i, j: (i, 0)),
            pl.BlockSpec((1, D), lambda i, j: (0, 0)),
            pl.BlockSpec((D, tn), lambda i, j: (0, j)),
            pl.BlockSpec((D, tn), lambda i, j: (0, j + nj)),
            pl.BlockSpec((1, tn), lambda i, j: (0, j)),
            pl.BlockSpec((1, tn), lambda i, j: (0, j + nj)),
        ],
        out_specs=pl.BlockSpec((tm, tn), lambda i, j: (i, j)),
        out_shape=jax.ShapeDtypeStruct((T, D), F32),
        scratch_shapes=[pltpu.VMEM((tm, D), BF16)],
        compiler_params=_params(("parallel", "arbitrary")),
        name="glu",
    )(x, g, w, w, b, b)


def _conv_out_kernel(halo_ref, u_ref, wdw_ref, bdw_ref, lng_ref, lnb_ref, wo_ref, bo_ref,
                     h_ref, o_ref, ext_ref, cv_ref, *, ts):
    i = pl.program_id(1)

    @pl.when(i == 0)
    def _():
        ext_ref[0:HALO, :] = jnp.zeros((HALO, D_MODEL), F32)

    @pl.when(i > 0)
    def _():
        ext_ref[0:HALO, :] = halo_ref[0]

    ext_ref[HALO:HALO + ts, :] = u_ref[0]

    base = HALO - (CONV_WIDTH - 1)

    def chan_body(c, carry):
        cs = pl.ds(pl.multiple_of(c * LANES, LANES), LANES)
        acc = jnp.zeros((ts, LANES), F32) + bdw_ref[:, cs]
        for k in range(CONV_WIDTH):
            acc = acc + ext_ref[pl.ds(base + k, ts), cs] * wdw_ref[k:k + 1, cs]
        cv_ref[:, cs] = acc
        return carry

    lax.fori_loop(0, D_MODEL // LANES, chan_body, 0)

    cv = cv_ref[...]
    mu = jnp.mean(cv, axis=-1, keepdims=True)
    xc = cv - mu
    var = jnp.mean(xc * xc, axis=-1, keepdims=True)
    y = xc * lax.rsqrt(var + LN_EPS) * lng_ref[...] + lnb_ref[...]
    y = (y * jax.nn.sigmoid(y)).astype(BF16)
    o_ref[...] = (h_ref[...] + bo_ref[...]
                  + jnp.dot(y, wo_ref[...], preferred_element_type=F32))


def _conv_out(u, h, w_dw, b_dw, ln_g, ln_b, w_out, b_out, *, batch, seq, ts=256):
    D = D_MODEL
    n = seq // ts
    u3 = u.reshape(batch, seq, D)
    hpb = ts // HALO
    row = lambda b, i: (b * n + i, 0)
    const = lambda b, i: (0, 0)
    return pl.pallas_call(
        functools.partial(_conv_out_kernel, ts=ts),
        grid=(batch, n),
        in_specs=[
            pl.BlockSpec((1, HALO, D), lambda b, i: (b, jnp.maximum(i * hpb - 1, 0), 0)),
            pl.BlockSpec((1, ts, D), lambda b, i: (b, i, 0)),
            pl.BlockSpec((CONV_WIDTH, D), const),
            pl.BlockSpec((1, D), const),
            pl.BlockSpec((1, D), const),
            pl.BlockSpec((1, D), const),
            pl.BlockSpec((D, D), const),
            pl.BlockSpec((1, D), const),
            pl.BlockSpec((ts, D), row),
        ],
        out_specs=pl.BlockSpec((ts, D), row),
        out_shape=jax.ShapeDtypeStruct((batch * seq, D), F32),
        scratch_shapes=[pltpu.VMEM((HALO + ts, D), F32), pltpu.VMEM((ts, D), F32)],
        compiler_params=_params(("parallel", "arbitrary")),
        name="conv_out",
    )(u3, u3, w_dw, b_dw, ln_g, ln_b, w_out, b_out, h)


def _mlp_kernel(h_ref, g_ref, w1_ref, w2_ref, gf_ref, o_ref, xn_ref, *, final_norm):
    f = pl.program_id(1)

    @pl.when(f == 0)
    def _():
        h = h_ref[...]
        xn_ref[...] = _rmsnorm_bf16(h, g_ref[...])
        o_ref[...] = h

    a = jnp.maximum(jnp.dot(xn_ref[...], w1_ref[...], preferred_element_type=F32), 0.0)
    o_ref[...] += jnp.dot((a * a).astype(BF16), w2_ref[...], preferred_element_type=F32)

    if final_norm:
        @pl.when(f == pl.num_programs(1) - 1)
        def _():
            h = o_ref[...]
            ms = jnp.mean(h * h, axis=-1, keepdims=True)
            o_ref[...] = h * lax.rsqrt(ms + NORM_EPS) * gf_ref[...]


def _mlp(h, g, w1, w2, gf, *, layer, final_norm, tm=512, tf=1024):
    T, D = h.shape
    return pl.pallas_call(
        functools.partial(_mlp_kernel, final_norm=final_norm),
        grid=(T // tm, D_FF // tf),
        in_specs=[
            pl.BlockSpec((tm, D), lambda i, f: (i, 0)),
            pl.BlockSpec((1, D), lambda i, f: (0, 0)),
            pl.BlockSpec((None, D, tf), lambda i, f: (layer, 0, f)),
            pl.BlockSpec((None, tf, D), lambda i, f: (layer, f, 0)),
            pl.BlockSpec((1, D), lambda i, f: (0, 0)),
        ],
        out_specs=pl.BlockSpec((tm, D), lambda i, f: (i, 0)),
        out_shape=jax.ShapeDtypeStruct((T, D), F32),
        scratch_shapes=[pltpu.VMEM((tm, D), BF16)],
        compiler_params=_params(("parallel", "arbitrary")),
        name="mlp",
    )(h, g, w1, w2, gf)


def _rope_tile(x, cos_t, sin_t):
    lane = lax.broadcasted_iota(jnp.int32, (x.shape[0], HEAD_DIM), 1)
    first_half = lane < ROT_DIM // 2
    outs = []
    for hh in range(x.shape[1] // HEAD_DIM):
        xh = x[:, hh * HEAD_DIM:(hh + 1) * HEAD_DIM]
        partner = jnp.where(first_half,
                            pltpu.roll(xh, HEAD_DIM - ROT_DIM // 2, 1),
                            pltpu.roll(xh, ROT_DIM // 2, 1))
        outs.append(xh * cos_t + partner * sin_t)
    return jnp.concatenate(outs, axis=1)


def _proj_kernel(x_ref, g_ref, w_ref, cos_ref, sin_ref, o_ref, xn_ref, *, n_rope, scale):
    j = pl.program_id(1)

    @pl.when(j == 0)
    def _():
        xn_ref[...] = _rmsnorm_bf16(x_ref[...], g_ref[...])

    acc = jnp.dot(xn_ref[...], w_ref[...], preferred_element_type=F32)
    heads = acc.shape[1] // HEAD_DIM

    def store(val):
        for hh in range(heads):
            o_ref[0, hh] = val[:, hh * HEAD_DIM:(hh + 1) * HEAD_DIM]

    @pl.when(j < n_rope)
    def _():
        store(_rope_tile(acc, cos_ref[...], sin_ref[...]) * scale)

    @pl.when(j >= n_rope)
    def _():
        store(acc)


def _proj(x, g, w, cos_t, sin_t, *, batch, seq, n_rope, scale, tm=512, tn=512):
    T, D = x.shape
    N = w.shape[1]
    spb = seq // tm
    hpt = tn // HEAD_DIM
    return pl.pallas_call(
        functools.partial(_proj_kernel, n_rope=n_rope, scale=scale),
        grid=(T // tm, N // tn),
        in_specs=[
            pl.BlockSpec((tm, D), lambda i, j: (i, 0)),
            pl.BlockSpec((1, D), lambda i, j: (0, 0)),
            pl.BlockSpec((D, tn), lambda i, j: (0, j)),
            pl.BlockSpec((tm, HEAD_DIM), lambda i, j: (i % spb, 0)),
            pl.BlockSpec((tm, HEAD_DIM), lambda i, j: (i % spb, 0)),
        ],
        out_specs=pl.BlockSpec((1, hpt, tm, HEAD_DIM), lambda i, j: (i // spb, j, i % spb, 0)),
        out_shape=jax.ShapeDtypeStruct((batch, N // HEAD_DIM, seq, HEAD_DIM), F32),
        scratch_shapes=[pltpu.VMEM((tm, D), BF16)],
        compiler_params=_params(("parallel", "arbitrary")),
        name="proj",
    )(x, g, w, cos_t, sin_t)


def _attn_kernel(q_ref, kp_ref, kc_ref, vp_ref, vc_ref, o_ref, st_ref, *, dil, steps, has_prev):
    n = pl.program_id(2)
    nk = 2 * BLOCK if has_prev else BLOCK
    c = lax.broadcasted_iota(jnp.int32, (GROUP * BLOCK, nk), 0) & (BLOCK - 1)
    j = lax.broadcasted_iota(jnp.int32, (GROUP * BLOCK, nk), 1)
    if has_prev:
        lo = c + (BLOCK - steps)
        j_min = jnp.where(n > 0, lo, jnp.maximum(lo, BLOCK))
        valid = jnp.logical_and(j >= j_min, j <= c + BLOCK)
    else:
        valid = jnp.logical_and(j <= c, j >= c - steps)
    lane = lax.broadcasted_iota(jnp.int32, (BLOCK, LANES), 1)

    def residue(r):
        rows = pl.ds(r, BLOCK, stride=dil) if dil > 1 else slice(None)
        if has_prev:
            kk = jnp.concatenate([kp_ref[0, 0, rows, :], kc_ref[0, 0, rows, :]], axis=0)
            vv = jnp.concatenate([vp_ref[0, 0, rows, :], vc_ref[0, 0, rows, :]], axis=0)
        else:
            kk = kc_ref[0, 0, rows, :]
            vv = vc_ref[0, 0, rows, :]
        qh = jnp.concatenate([q_ref[0, g, rows, :] for g in range(GROUP)], axis=0)
        s = lax.dot_general(qh.astype(BF16), kk.astype(BF16), (((1,), (1,)), ((), ())),
                            preferred_element_type=F32)
        s = jnp.where(valid, s, MASK_VALUE)
        m = jnp.max(s, axis=-1, keepdims=True)
        p = jnp.exp2(s - m)
        l = jnp.sum(p, axis=-1, keepdims=True)
        o = jnp.dot(p.astype(BF16), vv.astype(BF16), preferred_element_type=F32) / l
        lse2 = m + jnp.log2(l)
        stats = jnp.zeros((BLOCK, LANES), F32)
        for g in range(GROUP):
            sl = slice(g * BLOCK, (g + 1) * BLOCK)
            o_ref[0, g, rows, :] = o[sl]
            stats = jnp.where(lane == g, lse2[sl], stats)
        st_ref[0, 0, rows, :] = stats

    if dil > 1:
        def body(r, carry):
            residue(r)
            return carry
        lax.fori_loop(0, dil, body, 0)
    else:
        residue(0)


def _attn_branch(q, kv, *, dil, steps):
    batch, _, seq, _ = q.shape
    rows = dil * BLOCK
    nblk = seq // rows
    has_prev = nblk > 1
    prev = lambda n: jnp.maximum(n - 1, 0)
    kv_block = (1, 1, rows, HEAD_DIM)
    return pl.pallas_call(
        functools.partial(_attn_kernel, dil=dil, steps=steps, has_prev=has_prev),
        grid=(batch, N_KV_HEADS, nblk),
        in_specs=[
            pl.BlockSpec((1, GROUP, rows, HEAD_DIM), lambda b, h, n: (b, h, n, 0)),
            pl.BlockSpec(kv_block, lambda b, h, n: (b, h, prev(n), 0)),
            pl.BlockSpec(kv_block, lambda b, h, n: (b, h, n, 0)),
            pl.BlockSpec(kv_block, lambda b, h, n: (b, N_KV_HEADS + h, prev(n), 0)),
            pl.BlockSpec(kv_block, lambda b, h, n: (b, N_KV_HEADS + h, n, 0)),
        ],
        out_specs=[
            pl.BlockSpec((1, GROUP, rows, HEAD_DIM), lambda b, h, n: (b, h, n, 0)),
            pl.BlockSpec((1, 1, rows, LANES), lambda b, h, n: (b, h, n, 0)),
        ],
        out_shape=[
            jax.ShapeDtypeStruct((batch, N_HEADS, seq, HEAD_DIM), F32),
            jax.ShapeDtypeStruct((batch, N_KV_HEADS, seq, LANES), F32),
        ],
        compiler_params=_params(("parallel", "parallel", "arbitrary")),
        name=f"attn_d{dil}",
    )(q, kv, kv, kv, kv)


def _out_proj_kernel(o1_ref, o2_ref, o3_ref, s1_ref, s2_ref, s3_ref, w_ref, h_ref, out_ref, mix_ref):
    for kvh in range(N_KV_HEADS):
        l1, l2, l3 = s1_ref[0, kvh], s2_ref[0, kvh], s3_ref[0, kvh]
        mx = jnp.maximum(jnp.maximum(l1, l2), l3)
        e1, e2, e3 = jnp.exp2(l1 - mx), jnp.exp2(l2 - mx), jnp.exp2(l3 - mx)
        inv = 1.0 / (e1 + e2 + e3)
        w1, w2, w3 = e1 * inv, e2 * inv, e3 * inv
        for g in range(GROUP):
            head = kvh * GROUP + g
            gs = slice(g, g + 1)
            mix = (w1[:, gs] * o1_ref[0, head] + w2[:, gs] * o2_ref[0, head]
                   + w3[:, gs] * o3_ref[0, head])
            mix_ref[:, head * HEAD_DIM:(head + 1) * HEAD_DIM] = mix.astype(BF16)
    out_ref[...] = h_ref[...] + jnp.dot(mix_ref[...], w_ref[...], preferred_element_type=F32)


def _out_proj(os_, sts, w, h, *, seq, tm=256):
    T, D = h.shape
    qd = N_HEADS * HEAD_DIM
    spb = seq // tm
    row = lambda i: (i, 0)
    const = lambda i: (0, 0)
    head_major = lambda i: (i // spb, 0, i % spb, 0)
    return pl.pallas_call(
        _out_proj_kernel,
        grid=(T // tm,),
        in_specs=[pl.BlockSpec((1, N_HEADS, tm, HEAD_DIM), head_major)] * 3
        + [pl.BlockSpec((1, N_KV_HEADS, tm, LANES), head_major)] * 3
        + [pl.BlockSpec((qd, D), const), pl.BlockSpec((tm, D), row)],
        out_specs=pl.BlockSpec((tm, D), row),
        out_shape=jax.ShapeDtypeStruct((T, D), F32),
        scratch_shapes=[pltpu.VMEM((tm, qd), BF16)],
        compiler_params=_params(("parallel",)),
        name="out_proj",
    )(*os_, *sts, w, h)


def _rope_tables(seq):
    half = ROT_DIM // 2
    pos = jnp.arange(seq, dtype=F32)
    inv = ROPE_THETA ** (-jnp.arange(0, ROT_DIM, 2, dtype=F32) / ROT_DIM)
    ang = pos[:, None] * inv[None, :]
    cos, sin = jnp.cos(ang), jnp.sin(ang)
    cos_t = jnp.concatenate([cos, cos, jnp.ones((seq, HEAD_DIM - ROT_DIM), F32)], axis=1)
    sin_t = jnp.concatenate([-sin, sin, jnp.zeros((seq, HEAD_DIM - ROT_DIM), F32)], axis=1)
    return cos_t, sin_t


def kernel(x, norm_mix, norm_mlp, conv_w_in, conv_b_in, conv_w_dw, conv_b_dw, conv_ln_g, conv_ln_b,
           conv_w_out, conv_b_out, kv_norm, w_kv, attn_w_q, attn_w_o, mlp_w_in, mlp_w_out, final_norm):
    B, S, D = x.shape
    T = B * S
    assert D == D_MODEL and S % (max(DILATIONS) * BLOCK) == 0
    row = lambda v: v.reshape(1, -1)
    cos_t, sin_t = _rope_tables(S)
    w_mlp_in = mlp_w_in.astype(BF16)
    w_mlp_out = mlp_w_out.astype(BF16)

    h = x.reshape(T, D)

    u = _glu(h, row(norm_mix[0]), conv_w_in[0].astype(BF16), row(conv_b_in[0]))
    h = _conv_out(u, h, conv_w_dw[0], row(conv_b_dw[0]), row(conv_ln_g[0]), row(conv_ln_b[0]),
                  conv_w_out[0].astype(BF16), row(conv_b_out[0]), batch=B, seq=S)
    h = _mlp(h, row(norm_mlp[0]), w_mlp_in, w_mlp_out, row(final_norm), layer=0, final_norm=False)

    kv = _proj(h, row(kv_norm), w_kv.astype(BF16), cos_t, sin_t, batch=B, seq=S,
               n_rope=KV_DIM // 512, scale=1.0)

    q = _proj(h, row(norm_mix[1]), attn_w_q[0].astype(BF16), cos_t, sin_t, batch=B, seq=S,
              n_rope=(N_HEADS * HEAD_DIM) // 512, scale=LOG2_E / math.sqrt(HEAD_DIM))
    outs, stats = [], []
    for win, dil in zip(WINDOWS, DILATIONS):
        o, st = _attn_branch(q, kv, dil=dil, steps=win // dil)
        outs.append(o)
        stats.append(st)
    h = _out_proj(outs, stats, attn_w_o[0].astype(BF16), h, seq=S)
    h = _mlp(h, row(norm_mlp[1]), w_mlp_in, w_mlp_out, row(final_norm), layer=1, final_norm=True)
    return h.reshape(B, S, D)
```

```python
import functools
import math

import jax
import jax.numpy as jnp
from jax import lax
from jax.experimental import pallas as pl
from jax.experimental.pallas import tpu as pltpu

F32 = jnp.float32
BF16 = jnp.bfloat16

D_MODEL = 2048
CONV_WIDTH = 31
HEAD_DIM = 128
N_HEADS = 16
N_KV_HEADS = 4
GROUP = N_HEADS // N_KV_HEADS
KV_DIM = N_KV_HEADS * HEAD_DIM
WINDOWS = (128, 512, 2048)
DILATIONS = (1, 4, 16)
BLOCK = 128
ROPE_THETA = 500000.0
ROT_DIM = HEAD_DIM // 4
D_FF = 4 * D_MODEL
NORM_EPS = 1e-6
LN_EPS = 1e-5

LANES = 128
SUBLANES = 8
HALO = 32
VMEM_LIMIT = 56 * 1024 * 1024
MASK_VALUE = -1e30
LOG2_E = math.log2(math.e)
MAX_DIL = max(DILATIONS)
ATTN_UNROLL = 4


def _params(sem):
    return pltpu.CompilerParams(dimension_semantics=sem, vmem_limit_bytes=VMEM_LIMIT)


def _rmsnorm_bf16(x, g):
    ms = jnp.mean(x * x, axis=-1, keepdims=True)
    return (x * lax.rsqrt(ms + NORM_EPS) * g).astype(BF16)


def _glu_kernel(x_ref, g_ref, wa_ref, wg_ref, ba_ref, bg_ref, o_ref, xn_ref):
    @pl.when(pl.program_id(1) == 0)
    def _():
        xn_ref[...] = _rmsnorm_bf16(x_ref[...], g_ref[...])

    xn = xn_ref[...]
    a = jnp.dot(xn, wa_ref[...], preferred_element_type=F32) + ba_ref[...]
    gate = jnp.dot(xn, wg_ref[...], preferred_element_type=F32) + bg_ref[...]
    o_ref[...] = a * jax.nn.sigmoid(gate)


def _glu(x, g, w, b, *, tm=512, tn=512):
    T, D = x.shape
    nj = D // tn
    return pl.pallas_call(
        _glu_kernel,
        grid=(T // tm, nj),
        in_specs=[
            pl.BlockSpec((tm, D), lambda i, j: (i, 0)),
            pl.BlockSpec((1, D), lambda i, j: (0, 0)),
            pl.BlockSpec((D, tn), lambda i, j: (0, j)),
            pl.BlockSpec((D, tn), lambda i, j: (0, j + nj)),
            pl.BlockSpec((1, tn), lambda i, j: (0, j)),
            pl.BlockSpec((1, tn), lambda i, j: (0, j + nj)),
        ],
        out_specs=pl.BlockSpec((tm, tn), lambda i, j: (i, j)),
        out_shape=jax.ShapeDtypeStruct((T, D), F32),
        scratch_shapes=[pltpu.VMEM((tm, D), BF16)],
        compiler_params=_params(("parallel", "arbitrary")),
        name="glu",
    )(x, g, w, w, b, b)


def _conv_out_kernel(halo_ref, u_ref, wdw_ref, bdw_ref, lng_ref, lnb_ref, wo_ref, bo_ref,
                     h_ref, o_ref, ext_ref, cv_ref, *, ts):
    i = pl.program_id(1)

    @pl.when(i == 0)
    def _():
        ext_ref[0:HALO, :] = jnp.zeros((HALO, D_MODEL), F32)

    @pl.when(i > 0)
    def _():
        ext_ref[0:HALO, :] = halo_ref[0]

    ext_ref[HALO:HALO + ts, :] = u_ref[0]

    def chan_body(c, carry):
        cs = pl.ds(pl.multiple_of(c * LANES, LANES), LANES)
        acc = None
        for rho in range(SUBLANES):
            z = None
            for d in range(rho, CONV_WIDTH, SUBLANES):
                start = HALO - SUBLANES - (d - rho)
                k = CONV_WIDTH - 1 - d
                term = ext_ref[pl.ds(start, ts + SUBLANES), cs] * wdw_ref[k:k + 1, cs]
                z = term if z is None else z + term
            part = z[SUBLANES - rho:SUBLANES - rho + ts]
            acc = part if acc is None else acc + part
        cv_ref[:, cs] = acc + bdw_ref[:, cs]
        return carry

    lax.fori_loop(0, D_MODEL // LANES, chan_body, 0)

    cv = cv_ref[...]
    mu = jnp.mean(cv, axis=-1, keepdims=True)
    xc = cv - mu
    var = jnp.mean(xc * xc, axis=-1, keepdims=True)
    y = xc * lax.rsqrt(var + LN_EPS) * lng_ref[...] + lnb_ref[...]
    y = (y * jax.nn.sigmoid(y)).astype(BF16)
    o_ref[...] = (h_ref[...] + bo_ref[...]
                  + jnp.dot(y, wo_ref[...], preferred_element_type=F32))


def _conv_out(u, h, w_dw, b_dw, ln_g, ln_b, w_out, b_out, *, batch, seq, ts=256):
    D = D_MODEL
    n = seq // ts
    u3 = u.reshape(batch, seq, D)
    hpb = ts // HALO
    row = lambda b, i: (b * n + i, 0)
    const = lambda b, i: (0, 0)
    return pl.pallas_call(
        functools.partial(_conv_out_kernel, ts=ts),
        grid=(batch, n),
        in_specs=[
            pl.BlockSpec((1, HALO, D), lambda b, i: (b, jnp.maximum(i * hpb - 1, 0), 0)),
            pl.BlockSpec((1, ts, D), lambda b, i: (b, i, 0)),
            pl.BlockSpec((CONV_WIDTH, D), const),
            pl.BlockSpec((1, D), const),
            pl.BlockSpec((1, D), const),
            pl.BlockSpec((1, D), const),
            pl.BlockSpec((D, D), const),
            pl.BlockSpec((1, D), const),
            pl.BlockSpec((ts, D), row),
        ],
        out_specs=pl.BlockSpec((ts, D), row),
        out_shape=jax.ShapeDtypeStruct((batch * seq, D), F32),
        scratch_shapes=[pltpu.VMEM((HALO + ts, D), F32), pltpu.VMEM((ts, D), F32)],
        compiler_params=_params(("parallel", "arbitrary")),
        name="conv_out",
    )(u3, u3, w_dw, b_dw, ln_g, ln_b, w_out, b_out, h)


def _mlp_kernel(h_ref, g_ref, w1_ref, w2_ref, gf_ref, o_ref, xn_ref, *, final_norm):
    f = pl.program_id(1)

    @pl.when(f == 0)
    def _():
        h = h_ref[...]
        xn_ref[...] = _rmsnorm_bf16(h, g_ref[...])
        o_ref[...] = h

    a = jnp.maximum(jnp.dot(xn_ref[...], w1_ref[...], preferred_element_type=F32), 0.0)
    o_ref[...] += jnp.dot((a * a).astype(BF16), w2_ref[...], preferred_element_type=F32)

    if final_norm:
        @pl.when(f == pl.num_programs(1) - 1)
        def _():
            h = o_ref[...]
            ms = jnp.mean(h * h, axis=-1, keepdims=True)
            o_ref[...] = h * lax.rsqrt(ms + NORM_EPS) * gf_ref[...]


def _mlp(h, g, w1, w2, gf, *, layer, final_norm, tm=512, tf=1024):
    T, D = h.shape
    return pl.pallas_call(
        functools.partial(_mlp_kernel, final_norm=final_norm),
        grid=(T // tm, D_FF // tf),
        in_specs=[
            pl.BlockSpec((tm, D), lambda i, f: (i, 0)),
            pl.BlockSpec((1, D), lambda i, f: (0, 0)),
            pl.BlockSpec((None, D, tf), lambda i, f: (layer, 0, f)),
            pl.BlockSpec((None, tf, D), lambda i, f: (layer, f, 0)),
            pl.BlockSpec((1, D), lambda i, f: (0, 0)),
        ],
        out_specs=pl.BlockSpec((tm, D), lambda i, f: (i, 0)),
        out_shape=jax.ShapeDtypeStruct((T, D), F32),
        scratch_shapes=[pltpu.VMEM((tm, D), BF16)],
        compiler_params=_params(("parallel", "arbitrary")),
        name="mlp",
    )(h, g, w1, w2, gf)


def _rope_tile(x, cos_t, sin_t):
    lane = lax.broadcasted_iota(jnp.int32, (x.shape[0], HEAD_DIM), 1)
    first_half = lane < ROT_DIM // 2
    outs = []
    for hh in range(x.shape[1] // HEAD_DIM):
        xh = x[:, hh * HEAD_DIM:(hh + 1) * HEAD_DIM]
        partner = jnp.where(first_half,
                            pltpu.roll(xh, HEAD_DIM - ROT_DIM // 2, 1),
                            pltpu.roll(xh, ROT_DIM // 2, 1))
        outs.append(xh * cos_t + partner * sin_t)
    return jnp.concatenate(outs, axis=1)


def _proj_kernel(x_ref, g_ref, w_ref, cos_ref, sin_ref, o_ref, xn_ref, *, n_rope, scale):
    j = pl.program_id(1)

    @pl.when(j == 0)
    def _():
        xn_ref[...] = _rmsnorm_bf16(x_ref[...], g_ref[...])

    acc = jnp.dot(xn_ref[...], w_ref[...], preferred_element_type=F32)

    @pl.when(j < n_rope)
    def _():
        o_ref[...] = _rope_tile(acc, cos_ref[...], sin_ref[...]) * scale

    @pl.when(j >= n_rope)
    def _():
        o_ref[...] = acc


def _proj(x, g, w, cos_t, sin_t, *, seq, n_rope, scale, tm=512, tn=512):
    T, D = x.shape
    N = w.shape[1]
    spb = seq // tm
    return pl.pallas_call(
        functools.partial(_proj_kernel, n_rope=n_rope, scale=scale),
        grid=(T // tm, N // tn),
        in_specs=[
            pl.BlockSpec((tm, D), lambda i, j: (i, 0)),
            pl.BlockSpec((1, D), lambda i, j: (0, 0)),
            pl.BlockSpec((D, tn), lambda i, j: (0, j)),
            pl.BlockSpec((tm, HEAD_DIM), lambda i, j: (i % spb, 0)),
            pl.BlockSpec((tm, HEAD_DIM), lambda i, j: (i % spb, 0)),
        ],
        out_specs=pl.BlockSpec((tm, tn), lambda i, j: (i, j)),
        out_shape=jax.ShapeDtypeStruct((T, N), F32),
        scratch_shapes=[pltpu.VMEM((tm, D), BF16)],
        compiler_params=_params(("parallel", "arbitrary")),
        name="proj",
    )(x, g, w, cos_t, sin_t)


def _attn_kernel(q_ref, k_ref, v_ref, o_ref, acc_ref, m_ref, l_ref, *, seq):
    rpc = seq // MAX_DIL
    gq = GROUP * BLOCK

    def gather(ref, chunks, lanes):
        return jnp.concatenate([ref[0, pl.ds(st, sz), lanes] for st, sz in chunks], axis=0)

    def block(q_chunks, kv_chunks, dist, first, last):
        kk = gather(k_ref, kv_chunks, slice(None)).astype(BF16)
        vv = gather(v_ref, kv_chunks, slice(None)).astype(BF16)
        qh = jnp.concatenate(
            [gather(q_ref, q_chunks, slice(g * HEAD_DIM, (g + 1) * HEAD_DIM)) for g in range(GROUP)],
            axis=0).astype(BF16)
        s = lax.dot_general(qh, kk, (((1,), (1,)), ((), ())), preferred_element_type=F32)
        valid = jnp.logical_and(dist >= 0, dist <= BLOCK)
        s = jnp.where(valid, s, MASK_VALUE)
        m_b = jnp.broadcast_to(jnp.max(s, axis=-1, keepdims=True), (gq, LANES))

        def state(ref):
            return jnp.concatenate([gather(ref.at[g:g + 1], q_chunks, slice(None))
                                    for g in range(GROUP)], axis=0)

        if first:
            m_new = m_b
        else:
            m_old = state(m_ref)
            m_new = jnp.maximum(m_old, m_b)
        nk = s.shape[1]
        p = jnp.exp2(s - jnp.concatenate([m_new] * (nk // LANES), axis=1))
        l_b = jnp.broadcast_to(jnp.sum(p, axis=-1, keepdims=True), (gq, LANES))
        o_b = jnp.dot(p.astype(BF16), vv, preferred_element_type=F32)
        if first:
            acc, l = o_b, l_b
        else:
            alpha = jnp.exp2(m_old - m_new)
            acc = alpha * state(acc_ref) + o_b
            l = alpha * state(l_ref) + l_b

        if last:
            out = (acc / l).astype(o_ref.dtype)
            for g in range(GROUP):
                off = g * BLOCK
                for st, sz in q_chunks:
                    o_ref[0, pl.ds(st, sz), g * HEAD_DIM:(g + 1) * HEAD_DIM] = out[off:off + sz]
                    off += sz
        else:
            for g in range(GROUP):
                off = g * BLOCK
                for st, sz in q_chunks:
                    acc_ref[g, pl.ds(st, sz), :] = acc[off:off + sz]
                    m_ref[g, pl.ds(st, sz), :] = m_new[off:off + sz]
                    l_ref[g, pl.ds(st, sz), :] = l[off:off + sz]
                    off += sz

    def static_dist(nk, q_major_shift, k_major_shift, minor_scale):
        r = lax.broadcasted_iota(jnp.int32, (gq, nk), 0) & (BLOCK - 1)
        j = lax.broadcasted_iota(jnp.int32, (gq, nk), 1)
        qmaj, qmin = r >> q_major_shift, r & ((1 << q_major_shift) - 1)
        kmaj, kmin = j >> k_major_shift, j & ((1 << k_major_shift) - 1)
        return minor_scale * (qmin - kmin) + (qmaj - kmaj)

    def window_base(n):
        return jnp.where(n > 0, BLOCK, 0)

    per1 = BLOCK // MAX_DIL
    d1 = static_dist(2 * BLOCK, 3, 4, MAX_DIL)

    def body1(nb, carry):
        q0 = pl.multiple_of(nb * per1, per1)
        k0 = pl.multiple_of(jnp.maximum(nb - 1, 0) * per1, per1)
        block([(c * rpc + q0, per1) for c in range(MAX_DIL)],
              [(c * rpc + k0, 2 * per1) for c in range(MAX_DIL)],
              d1 + window_base(nb), first=True, last=False)
        return carry

    lax.fori_loop(0, seq // BLOCK, body1, 0, unroll=ATTN_UNROLL)

    dil = DILATIONS[1]
    sub = MAX_DIL // dil
    per4 = BLOCK // sub
    d4 = static_dist(2 * BLOCK, 5, 6, sub)
    nblk4 = seq // (dil * BLOCK)

    def body4(idx, carry):
        r4, n = idx // nblk4, idx % nblk4
        q0 = pl.multiple_of(n * per4, per4)
        k0 = pl.multiple_of(jnp.maximum(n - 1, 0) * per4, per4)
        block([((r4 + dil * c) * rpc + q0, per4) for c in range(sub)],
              [((r4 + dil * c) * rpc + k0, 2 * per4) for c in range(sub)],
              d4 + window_base(n), first=False, last=False)
        return carry

    lax.fori_loop(0, dil * nblk4, body4, 0, unroll=ATTN_UNROLL)

    d16 = static_dist(BLOCK, 7, 7, 1)

    def body16(c, carry):
        r0 = pl.multiple_of(c * rpc, rpc)
        block([(r0, BLOCK)], [(r0, BLOCK)], d16, first=False, last=True)
        return carry

    lax.fori_loop(0, MAX_DIL, body16, 0, unroll=ATTN_UNROLL)


def _attention(q, kv, *, batch, seq):
    assert DILATIONS == (1, 4, 16) and all(w // d == BLOCK for w, d in zip(WINDOWS, DILATIONS))
    assert seq == MAX_DIL * BLOCK
    qd = N_HEADS * HEAD_DIM
    q3 = q.reshape(batch, seq, qd)
    kv3 = kv.reshape(batch, seq, 2 * KV_DIM)
    gw = GROUP * HEAD_DIM
    o = pl.pallas_call(
        functools.partial(_attn_kernel, seq=seq),
        grid=(batch, N_KV_HEADS),
        in_specs=[
            pl.BlockSpec((1, seq, gw), lambda b, h: (b, 0, h)),
            pl.BlockSpec((1, seq, HEAD_DIM), lambda b, h: (b, 0, h)),
            pl.BlockSpec((1, seq, HEAD_DIM), lambda b, h: (b, 0, N_KV_HEADS + h)),
        ],
        out_specs=pl.BlockSpec((1, seq, gw), lambda b, h: (b, 0, h)),
        out_shape=jax.ShapeDtypeStruct((batch, seq, qd), BF16),
        scratch_shapes=[pltpu.VMEM((GROUP, seq, LANES), F32)] * 3,
        compiler_params=_params(("parallel", "parallel")),
        name="attn",
    )(q3, kv3, kv3)
    return o.reshape(batch * seq, qd)


def _out_proj_kernel(o_ref, w_ref, h_ref, out_ref):
    out_ref[...] = h_ref[...] + jnp.dot(o_ref[...], w_ref[...], preferred_element_type=F32)


def _out_proj(o, w, h, *, tm=512):
    T, D = h.shape
    qd = N_HEADS * HEAD_DIM
    row = lambda i: (i, 0)
    return pl.pallas_call(
        _out_proj_kernel,
        grid=(T // tm,),
        in_specs=[pl.BlockSpec((tm, qd), row), pl.BlockSpec((qd, D), lambda i: (0, 0)),
                  pl.BlockSpec((tm, D), row)],
        out_specs=pl.BlockSpec((tm, D), row),
        out_shape=jax.ShapeDtypeStruct((T, D), F32),
        compiler_params=_params(("parallel",)),
        name="out_proj",
    )(o, w, h)


def _rope_tables(pos):
    seq = pos.shape[0]
    inv = ROPE_THETA ** (-jnp.arange(0, ROT_DIM, 2, dtype=F32) / ROT_DIM)
    ang = pos[:, None] * inv[None, :]
    cos, sin = jnp.cos(ang), jnp.sin(ang)
    cos_t = jnp.concatenate([cos, cos, jnp.ones((seq, HEAD_DIM - ROT_DIM), F32)], axis=1)
    sin_t = jnp.concatenate([-sin, sin, jnp.zeros((seq, HEAD_DIM - ROT_DIM), F32)], axis=1)
    return cos_t, sin_t


def kernel(x, norm_mix, norm_mlp, conv_w_in, conv_b_in, conv_w_dw, conv_b_dw, conv_ln_g, conv_ln_b,
           conv_w_out, conv_b_out, kv_norm, w_kv, attn_w_q, attn_w_o, mlp_w_in, mlp_w_out, final_norm):
    B, S, D = x.shape
    T = B * S
    assert D == D_MODEL
    row = lambda v: v.reshape(1, -1)
    w_mlp_in = mlp_w_in.astype(BF16)
    w_mlp_out = mlp_w_out.astype(BF16)
    rows = jnp.arange(S)
    pos = MAX_DIL * (rows % (S // MAX_DIL)) + rows // (S // MAX_DIL)
    cos_t, sin_t = _rope_tables(pos.astype(F32))

    h = x.reshape(T, D)

    u = _glu(h, row(norm_mix[0]), conv_w_in[0].astype(BF16), row(conv_b_in[0]))
    h = _conv_out(u, h, conv_w_dw[0], row(conv_b_dw[0]), row(conv_ln_g[0]), row(conv_ln_b[0]),
                  conv_w_out[0].astype(BF16), row(conv_b_out[0]), batch=B, seq=S)
    h = _mlp(h, row(norm_mlp[0]), w_mlp_in, w_mlp_out, row(final_norm), layer=0, final_norm=False)

    h = h.reshape(B, S // MAX_DIL, MAX_DIL, D).transpose(0, 2, 1, 3).reshape(T, D)

    kv = _proj(h, row(kv_norm), w_kv.astype(BF16), cos_t, sin_t, seq=S,
               n_rope=KV_DIM // 512, scale=1.0)

    q = _proj(h, row(norm_mix[1]), attn_w_q[0].astype(BF16), cos_t, sin_t, seq=S,
              n_rope=(N_HEADS * HEAD_DIM) // 512, scale=LOG2_E / math.sqrt(HEAD_DIM))
    o = _attention(q, kv, batch=B, seq=S)
    h = _out_proj(o, attn_w_o[0].astype(BF16), h)
    h = _mlp(h, row(norm_mlp[1]), w_mlp_in, w_mlp_out, row(final_norm), layer=1, final_norm=True)
    return h.reshape(B, MAX_DIL, S // MAX_DIL, D).transpose(0, 2, 1, 3).reshape(B, S, D)
```

```python
import functools
import math

import jax
import jax.numpy as jnp
from jax import lax
from jax.experimental import pallas as pl
from jax.experimental.pallas import tpu as pltpu

F32 = jnp.float32
BF16 = jnp.bfloat16

D_MODEL = 2048
CONV_WIDTH = 31
HEAD_DIM = 128
N_HEADS = 16
N_KV_HEADS = 4
GROUP = N_HEADS // N_KV_HEADS
KV_DIM = N_KV_HEADS * HEAD_DIM
WINDOWS = (128, 512, 2048)
DILATIONS = (1, 4, 16)
BLOCK = 128
ROPE_THETA = 500000.0
ROT_DIM = HEAD_DIM // 4
D_FF = 4 * D_MODEL
NORM_EPS = 1e-6
LN_EPS = 1e-5

LANES = 128
SUBLANES = 8
HALO = 32
VMEM_LIMIT = 56 * 1024 * 1024
MASK_VALUE = -1e30
LOG2_E = math.log2(math.e)
MAX_DIL = max(DILATIONS)
ATTN_UNROLL = 4


def _params(sem):
    return pltpu.CompilerParams(dimension_semantics=sem, vmem_limit_bytes=VMEM_LIMIT)


def _rmsnorm_bf16(x, g):
    ms = jnp.mean(x * x, axis=-1, keepdims=True)
    return (x * lax.rsqrt(ms + NORM_EPS) * g).astype(BF16)


def _conv_mixer_kernel(x_ref, g_ref, win_ref, bin_ref, wdw_ref, bdw_ref, lng_ref, lnb_ref, wo_ref, bo_ref,
                       o_ref, ext_ref, cv_ref, *, ts, tn):
    i = pl.program_id(1)

    @pl.when(i == 0)
    def _():
        ext_ref[0:HALO, :] = jnp.zeros((HALO, D_MODEL), F32)

    @pl.when(i > 0)
    def _():
        ext_ref[0:HALO, :] = ext_ref[ts:ts + HALO, :]

    x = x_ref[...]
    xn = _rmsnorm_bf16(x, g_ref[...])

    for c in range(D_MODEL // tn):
        cs = slice(c * tn, (c + 1) * tn)
        gs = slice(D_MODEL + c * tn, D_MODEL + (c + 1) * tn)
        a = jnp.dot(xn, win_ref[:, cs], preferred_element_type=F32) + bin_ref[:, cs]
        gate = jnp.dot(xn, win_ref[:, gs], preferred_element_type=F32) + bin_ref[:, gs]
        ext_ref[HALO:HALO + ts, cs] = a * jax.nn.sigmoid(gate)

        for lt in range(c * tn // LANES, (c + 1) * tn // LANES):
            ls = slice(lt * LANES, (lt + 1) * LANES)
            acc = None
            for rho in range(SUBLANES):
                z = None
                for d in range(rho, CONV_WIDTH, SUBLANES):
                    start = HALO - SUBLANES - (d - rho)
                    k = CONV_WIDTH - 1 - d
                    term = ext_ref[start:start + ts + SUBLANES, ls] * wdw_ref[k:k + 1, ls]
                    z = term if z is None else z + term
                part = z[SUBLANES - rho:SUBLANES - rho + ts]
                acc = part if acc is None else acc + part
            cv_ref[:, ls] = acc + bdw_ref[:, ls]

    cv = cv_ref[...]
    mu = jnp.mean(cv, axis=-1, keepdims=True)
    xc = cv - mu
    var = jnp.mean(xc * xc, axis=-1, keepdims=True)
    y = xc * lax.rsqrt(var + LN_EPS) * lng_ref[...] + lnb_ref[...]
    y = (y * jax.nn.sigmoid(y)).astype(BF16)
    o_ref[...] = x + bo_ref[...] + jnp.dot(y, wo_ref[...], preferred_element_type=F32)


def _conv_mixer(h, g, w_in, b_in, w_dw, b_dw, ln_g, ln_b, w_out, b_out, *, batch, seq, ts=256, tn=256):
    D = D_MODEL
    n = seq // ts
    row = lambda b, i: (b * n + i, 0)
    const = lambda b, i: (0, 0)
    resident = lambda shape: pl.BlockSpec(shape, const, pipeline_mode=pl.Buffered(1))
    return pl.pallas_call(
        functools.partial(_conv_mixer_kernel, ts=ts, tn=tn),
        grid=(batch, n),
        in_specs=[
            pl.BlockSpec((ts, D), row),
            resident((1, D)),
            resident((D, 2 * D)),
            resident((1, 2 * D)),
            resident((CONV_WIDTH, D)),
            resident((1, D)),
            resident((1, D)),
            resident((1, D)),
            resident((D, D)),
            resident((1, D)),
        ],
        out_specs=pl.BlockSpec((ts, D), row),
        out_shape=jax.ShapeDtypeStruct((batch * seq, D), F32),
        scratch_shapes=[pltpu.VMEM((HALO + ts, D), F32), pltpu.VMEM((ts, D), F32)],
        compiler_params=_params(("parallel", "arbitrary")),
        name="conv_mixer",
    )(h, g, w_in, b_in, w_dw, b_dw, ln_g, ln_b, w_out, b_out)


def _mlp_kernel(h_ref, g_ref, w1_ref, w2_ref, gf_ref, o_ref, xn_ref, *, final_norm):
    f = pl.program_id(1)

    @pl.when(f == 0)
    def _():
        h = h_ref[...]
        xn_ref[...] = _rmsnorm_bf16(h, g_ref[...])
        o_ref[...] = h

    a = jnp.maximum(jnp.dot(xn_ref[...], w1_ref[...], preferred_element_type=F32), 0.0)
    o_ref[...] += jnp.dot((a * a).astype(BF16), w2_ref[...], preferred_element_type=F32)

    if final_norm:
        @pl.when(f == pl.num_programs(1) - 1)
        def _():
            h = o_ref[...]
            ms = jnp.mean(h * h, axis=-1, keepdims=True)
            o_ref[...] = h * lax.rsqrt(ms + NORM_EPS) * gf_ref[...]


def _mlp(h, g, w1, w2, gf, *, layer, final_norm, tm=512, tf=1024):
    T, D = h.shape
    return pl.pallas_call(
        functools.partial(_mlp_kernel, final_norm=final_norm),
        grid=(T // tm, D_FF // tf),
        in_specs=[
            pl.BlockSpec((tm, D), lambda i, f: (i, 0)),
            pl.BlockSpec((1, D), lambda i, f: (0, 0)),
            pl.BlockSpec((None, D, tf), lambda i, f: (layer, 0, f)),
            pl.BlockSpec((None, tf, D), lambda i, f: (layer, f, 0)),
            pl.BlockSpec((1, D), lambda i, f: (0, 0)),
        ],
        out_specs=pl.BlockSpec((tm, D), lambda i, f: (i, 0)),
        out_shape=jax.ShapeDtypeStruct((T, D), F32),
        scratch_shapes=[pltpu.VMEM((tm, D), BF16)],
        compiler_params=_params(("parallel", "arbitrary")),
        name="mlp",
    )(h, g, w1, w2, gf)


def _rope_tile(x, cos_t, sin_t):
    lane = lax.broadcasted_iota(jnp.int32, (x.shape[0], HEAD_DIM), 1)
    first_half = lane < ROT_DIM // 2
    outs = []
    for hh in range(x.shape[1] // HEAD_DIM):
        xh = x[:, hh * HEAD_DIM:(hh + 1) * HEAD_DIM]
        partner = jnp.where(first_half,
                            pltpu.roll(xh, HEAD_DIM - ROT_DIM // 2, 1),
                            pltpu.roll(xh, ROT_DIM // 2, 1))
        outs.append(xh * cos_t + partner * sin_t)
    return jnp.concatenate(outs, axis=1)


def _qkv_kernel(x_ref, gq_ref, gkv_ref, wq_ref, wkv_ref, cos_ref, sin_ref, q_ref, kv_ref,
                *, q_scale, sub, tn):
    tm = x_ref.shape[0]
    for r in range(tm // sub):
        rs = slice(r * sub, (r + 1) * sub)
        x = x_ref[rs, :]
        xs = x * lax.rsqrt(jnp.mean(x * x, axis=-1, keepdims=True) + NORM_EPS)
        xq = (xs * gq_ref[...]).astype(BF16)
        xkv = (xs * gkv_ref[...]).astype(BF16)
        cos_t, sin_t = cos_ref[rs, :], sin_ref[rs, :]
        for c in range(q_ref.shape[1] // tn):
            cs = slice(c * tn, (c + 1) * tn)
            acc = jnp.dot(xq, wq_ref[:, cs], preferred_element_type=F32)
            q_ref[rs, cs] = _rope_tile(acc, cos_t, sin_t) * q_scale
        k = jnp.dot(xkv, wkv_ref[:, :KV_DIM], preferred_element_type=F32)
        kv_ref[rs, :KV_DIM] = _rope_tile(k, cos_t, sin_t)
        kv_ref[rs, KV_DIM:] = jnp.dot(xkv, wkv_ref[:, KV_DIM:], preferred_element_type=F32)


def _qkv_proj(x, g_q, g_kv, w_q, w_kv, cos_t, sin_t, *, seq, q_scale, tm=512, sub=256, tn=512):
    T, D = x.shape
    qd, kvd = w_q.shape[1], w_kv.shape[1]
    spb = seq // tm
    row = lambda i: (i, 0)
    const = lambda i: (0, 0)
    table = pl.BlockSpec((tm, HEAD_DIM), lambda i: (i % spb, 0))
    return pl.pallas_call(
        functools.partial(_qkv_kernel, q_scale=q_scale, sub=sub, tn=tn),
        grid=(T // tm,),
        in_specs=[
            pl.BlockSpec((tm, D), row),
            pl.BlockSpec((1, D), const),
            pl.BlockSpec((1, D), const),
            pl.BlockSpec((D, qd), const),
            pl.BlockSpec((D, kvd), const),
            table, table,
        ],
        out_specs=[pl.BlockSpec((tm, qd), row), pl.BlockSpec((tm, kvd), row)],
        out_shape=[jax.ShapeDtypeStruct((T, qd), F32), jax.ShapeDtypeStruct((T, kvd), F32)],
        compiler_params=_params(("parallel",)),
        name="qkv_proj",
    )(x, g_q, g_kv, w_q, w_kv, cos_t, sin_t)


def _attn_kernel(q_ref, k_ref, v_ref, o_ref, acc_ref, m_ref, l_ref, *, seq):
    rpc = seq // MAX_DIL
    gq = GROUP * BLOCK

    def gather(ref, chunks, lanes):
        return jnp.concatenate([ref[0, pl.ds(st, sz), lanes] for st, sz in chunks], axis=0)

    def block(q_chunks, kv_chunks, dist, first, last):
        kk = gather(k_ref, kv_chunks, slice(None)).astype(BF16)
        vv = gather(v_ref, kv_chunks, slice(None)).astype(BF16)
        qh = jnp.concatenate(
            [gather(q_ref, q_chunks, slice(g * HEAD_DIM, (g + 1) * HEAD_DIM)) for g in range(GROUP)],
            axis=0).astype(BF16)
        s = lax.dot_general(qh, kk, (((1,), (1,)), ((), ())), preferred_element_type=F32)
        valid = jnp.logical_and(dist >= 0, dist <= BLOCK)
        s = jnp.where(valid, s, MASK_VALUE)
        m_b = jnp.broadcast_to(jnp.max(s, axis=-1, keepdims=True), (gq, LANES))

        def state(ref):
            return jnp.concatenate([gather(ref.at[g:g + 1], q_chunks, slice(None))
                                    for g in range(GROUP)], axis=0)

        if first:
            m_new = m_b
        else:
            m_old = state(m_ref)
            m_new = jnp.maximum(m_old, m_b)
        nk = s.shape[1]
        p = jnp.exp2(s - jnp.concatenate([m_new] * (nk // LANES), axis=1))
        l_b = jnp.broadcast_to(jnp.sum(p, axis=-1, keepdims=True), (gq, LANES))
        o_b = jnp.dot(p.astype(BF16), vv, preferred_element_type=F32)
        if first:
            acc, l = o_b, l_b
        else:
            alpha = jnp.exp2(m_old - m_new)
            acc = alpha * state(acc_ref) + o_b
            l = alpha * state(l_ref) + l_b

        if last:
            out = (acc / l).astype(o_ref.dtype)
            for g in range(GROUP):
                off = g * BLOCK
                for st, sz in q_chunks:
                    o_ref[0, pl.ds(st, sz), g * HEAD_DIM:(g + 1) * HEAD_DIM] = out[off:off + sz]
                    off += sz
        else:
            for g in range(GROUP):
                off = g * BLOCK
                for st, sz in q_chunks:
                    acc_ref[g, pl.ds(st, sz), :] = acc[off:off + sz]
                    m_ref[g, pl.ds(st, sz), :] = m_new[off:off + sz]
                    l_ref[g, pl.ds(st, sz), :] = l[off:off + sz]
                    off += sz

    def static_dist(nk, q_major_shift, k_major_shift, minor_scale):
        r = lax.broadcasted_iota(jnp.int32, (gq, nk), 0) & (BLOCK - 1)
        j = lax.broadcasted_iota(jnp.int32, (gq, nk), 1)
        qmaj, qmin = r >> q_major_shift, r & ((1 << q_major_shift) - 1)
        kmaj, kmin = j >> k_major_shift, j & ((1 << k_major_shift) - 1)
        return minor_scale * (qmin - kmin) + (qmaj - kmaj)

    def window_base(n):
        return jnp.where(n > 0, BLOCK, 0)

    per1 = BLOCK // MAX_DIL
    d1 = static_dist(2 * BLOCK, 3, 4, MAX_DIL)

    def body1(nb, carry):
        q0 = pl.multiple_of(nb * per1, per1)
        k0 = pl.multiple_of(jnp.maximum(nb - 1, 0) * per1, per1)
        block([(c * rpc + q0, per1) for c in range(MAX_DIL)],
              [(c * rpc + k0, 2 * per1) for c in range(MAX_DIL)],
              d1 + window_base(nb), first=True, last=False)
        return carry

    lax.fori_loop(0, seq // BLOCK, body1, 0, unroll=ATTN_UNROLL)

    dil = DILATIONS[1]
    sub = MAX_DIL // dil
    per4 = BLOCK // sub
    d4 = static_dist(2 * BLOCK, 5, 6, sub)
    nblk4 = seq // (dil * BLOCK)

    def body4(idx, carry):
        r4, n = idx // nblk4, idx % nblk4
        q0 = pl.multiple_of(n * per4, per4)
        k0 = pl.multiple_of(jnp.maximum(n - 1, 0) * per4, per4)
        block([((r4 + dil * c) * rpc + q0, per4) for c in range(sub)],
              [((r4 + dil * c) * rpc + k0, 2 * per4) for c in range(sub)],
              d4 + window_base(n), first=False, last=False)
        return carry

    lax.fori_loop(0, dil * nblk4, body4, 0, unroll=ATTN_UNROLL)

    d16 = static_dist(BLOCK, 7, 7, 1)

    def body16(c, carry):
        r0 = pl.multiple_of(c * rpc, rpc)
        block([(r0, BLOCK)], [(r0, BLOCK)], d16, first=False, last=True)
        return carry

    lax.fori_loop(0, MAX_DIL, body16, 0, unroll=ATTN_UNROLL)


def _attention(q, kv, *, batch, seq):
    assert DILATIONS == (1, 4, 16) and all(w // d == BLOCK for w, d in zip(WINDOWS, DILATIONS))
    assert seq == MAX_DIL * BLOCK
    qd = N_HEADS * HEAD_DIM
    q3 = q.reshape(batch, seq, qd)
    kv3 = kv.reshape(batch, seq, 2 * KV_DIM)
    gw = GROUP * HEAD_DIM
    o = pl.pallas_call(
        functools.partial(_attn_kernel, seq=seq),
        grid=(batch, N_KV_HEADS),
        in_specs=[
            pl.BlockSpec((1, seq, gw), lambda b, h: (b, 0, h)),
            pl.BlockSpec((1, seq, HEAD_DIM), lambda b, h: (b, 0, h)),
            pl.BlockSpec((1, seq, HEAD_DIM), lambda b, h: (b, 0, N_KV_HEADS + h)),
        ],
        out_specs=pl.BlockSpec((1, seq, gw), lambda b, h: (b, 0, h)),
        out_shape=jax.ShapeDtypeStruct((batch, seq, qd), BF16),
        scratch_shapes=[pltpu.VMEM((GROUP, seq, LANES), F32)] * 3,
        compiler_params=_params(("parallel", "parallel")),
        name="attn",
    )(q3, kv3, kv3)
    return o.reshape(batch * seq, qd)


def _out_proj_kernel(o_ref, w_ref, h_ref, out_ref):
    out_ref[...] = h_ref[...] + jnp.dot(o_ref[...], w_ref[...], preferred_element_type=F32)


def _out_proj(o, w, h, *, tm=512):
    T, D = h.shape
    qd = N_HEADS * HEAD_DIM
    row = lambda i: (i, 0)
    return pl.pallas_call(
        _out_proj_kernel,
        grid=(T // tm,),
        in_specs=[pl.BlockSpec((tm, qd), row), pl.BlockSpec((qd, D), lambda i: (0, 0)),
                  pl.BlockSpec((tm, D), row)],
        out_specs=pl.BlockSpec((tm, D), row),
        out_shape=jax.ShapeDtypeStruct((T, D), F32),
        compiler_params=_params(("parallel",)),
        name="out_proj",
    )(o, w, h)


def _rope_tables(pos):
    seq = pos.shape[0]
    inv = ROPE_THETA ** (-jnp.arange(0, ROT_DIM, 2, dtype=F32) / ROT_DIM)
    ang = pos[:, None] * inv[None, :]
    cos, sin = jnp.cos(ang), jnp.sin(ang)
    cos_t = jnp.concatenate([cos, cos, jnp.ones((seq, HEAD_DIM - ROT_DIM), F32)], axis=1)
    sin_t = jnp.concatenate([-sin, sin, jnp.zeros((seq, HEAD_DIM - ROT_DIM), F32)], axis=1)
    return cos_t, sin_t


def kernel(x, norm_mix, norm_mlp, conv_w_in, conv_b_in, conv_w_dw, conv_b_dw, conv_ln_g, conv_ln_b,
           conv_w_out, conv_b_out, kv_norm, w_kv, attn_w_q, attn_w_o, mlp_w_in, mlp_w_out, final_norm):
    B, S, D = x.shape
    T = B * S
    assert D == D_MODEL
    row = lambda v: v.reshape(1, -1)
    w_mlp_in = mlp_w_in.astype(BF16)
    w_mlp_out = mlp_w_out.astype(BF16)
    rows = jnp.arange(S)
    pos = MAX_DIL * (rows % (S // MAX_DIL)) + rows // (S // MAX_DIL)
    cos_t, sin_t = _rope_tables(pos.astype(F32))

    h = x.reshape(T, D)

    h = _conv_mixer(h, row(norm_mix[0]), conv_w_in[0].astype(BF16), row(conv_b_in[0]), conv_w_dw[0],
                    row(conv_b_dw[0]), row(conv_ln_g[0]), row(conv_ln_b[0]),
                    conv_w_out[0].astype(BF16), row(conv_b_out[0]), batch=B, seq=S)
    h = _mlp(h, row(norm_mlp[0]), w_mlp_in, w_mlp_out, row(final_norm), layer=0, final_norm=False)

    h = h.reshape(B, S // MAX_DIL, MAX_DIL, D).transpose(0, 2, 1, 3).reshape(T, D)

    q, kv = _qkv_proj(h, row(norm_mix[1]), row(kv_norm), attn_w_q[0].astype(BF16), w_kv.astype(BF16),
                      cos_t, sin_t, seq=S, q_scale=LOG2_E / math.sqrt(HEAD_DIM))

    o = _attention(q, kv, batch=B, seq=S)
    h = _out_proj(o, attn_w_o[0].astype(BF16), h)
    h = _mlp(h, row(norm_mlp[1]), w_mlp_in, w_mlp_out, row(final_norm), layer=1, final_norm=True)
    return h.reshape(B, MAX_DIL, S // MAX_DIL, D).transpose(0, 2, 1, 3).reshape(B, S, D)
```

```python
import functools
import math

import jax
import jax.numpy as jnp
from jax import lax
from jax.experimental import pallas as pl
from jax.experimental.pallas import tpu as pltpu

F32 = jnp.float32
BF16 = jnp.bfloat16

D_MODEL = 2048
CONV_WIDTH = 31
HEAD_DIM = 128
N_HEADS = 16
N_KV_HEADS = 4
GROUP = N_HEADS // N_KV_HEADS
KV_DIM = N_KV_HEADS * HEAD_DIM
WINDOWS = (128, 512, 2048)
DILATIONS = (1, 4, 16)
BLOCK = 128
ROPE_THETA = 500000.0
ROT_DIM = HEAD_DIM // 4
D_FF = 4 * D_MODEL
NORM_EPS = 1e-6
LN_EPS = 1e-5

LANES = 128
SUBLANES = 8
HALO = 32
VMEM_LIMIT = 56 * 1024 * 1024
MASK_VALUE = -1e30
LOG2_E = math.log2(math.e)
MAX_DIL = max(DILATIONS)
ATTN_UNROLL = 16


def _params(sem):
    return pltpu.CompilerParams(dimension_semantics=sem, vmem_limit_bytes=VMEM_LIMIT)


def _rmsnorm_bf16(x, g):
    ms = jnp.mean(x * x, axis=-1, keepdims=True)
    return (x * lax.rsqrt(ms + NORM_EPS) * g).astype(BF16)


def _conv_mixer_kernel(x_ref, g_ref, win_ref, bin_ref, wdw_ref, bdw_ref, lng_ref, lnb_ref, wo_ref, bo_ref,
                       o_ref, ext_ref, cv_ref, *, ts, tn):
    i = pl.program_id(1)

    @pl.when(i == 0)
    def _():
        ext_ref[0:HALO, :] = jnp.zeros((HALO, D_MODEL), F32)

    @pl.when(i > 0)
    def _():
        ext_ref[0:HALO, :] = ext_ref[ts:ts + HALO, :]

    x = x_ref[...]
    xn = _rmsnorm_bf16(x, g_ref[...])

    for c in range(D_MODEL // tn):
        cs = slice(c * tn, (c + 1) * tn)
        gs = slice(D_MODEL + c * tn, D_MODEL + (c + 1) * tn)
        a = jnp.dot(xn, win_ref[:, cs], preferred_element_type=F32) + bin_ref[:, cs]
        gate = jnp.dot(xn, win_ref[:, gs], preferred_element_type=F32) + bin_ref[:, gs]
        ext_ref[HALO:HALO + ts, cs] = a * jax.nn.sigmoid(gate)

        for lt in range(c * tn // LANES, (c + 1) * tn // LANES):
            ls = slice(lt * LANES, (lt + 1) * LANES)
            acc = None
            for rho in range(SUBLANES):
                z = None
                for d in range(rho, CONV_WIDTH, SUBLANES):
                    start = HALO - SUBLANES - (d - rho)
                    k = CONV_WIDTH - 1 - d
                    term = ext_ref[start:start + ts + SUBLANES, ls] * wdw_ref[k:k + 1, ls]
                    z = term if z is None else z + term
                part = z[SUBLANES - rho:SUBLANES - rho + ts]
                acc = part if acc is None else acc + part
            cv_ref[:, ls] = acc + bdw_ref[:, ls]

    cv = cv_ref[...]
    mu = jnp.mean(cv, axis=-1, keepdims=True)
    xc = cv - mu
    var = jnp.mean(xc * xc, axis=-1, keepdims=True)
    y = xc * lax.rsqrt(var + LN_EPS) * lng_ref[...] + lnb_ref[...]
    y = (y * jax.nn.sigmoid(y)).astype(BF16)
    o_ref[...] = x + bo_ref[...] + jnp.dot(y, wo_ref[...], preferred_element_type=F32)


def _conv_mixer(h, g, w_in, b_in, w_dw, b_dw, ln_g, ln_b, w_out, b_out, *, batch, seq, ts=256, tn=256):
    D = D_MODEL
    n = seq // ts
    row = lambda b, i: (b * n + i, 0)
    const = lambda b, i: (0, 0)
    resident = lambda shape: pl.BlockSpec(shape, const, pipeline_mode=pl.Buffered(1))
    return pl.pallas_call(
        functools.partial(_conv_mixer_kernel, ts=ts, tn=tn),
        grid=(batch, n),
        in_specs=[
            pl.BlockSpec((ts, D), row),
            resident((1, D)),
            resident((D, 2 * D)),
            resident((1, 2 * D)),
            resident((CONV_WIDTH, D)),
            resident((1, D)),
            resident((1, D)),
            resident((1, D)),
            resident((D, D)),
            resident((1, D)),
        ],
        out_specs=pl.BlockSpec((ts, D), row),
        out_shape=jax.ShapeDtypeStruct((batch * seq, D), F32),
        scratch_shapes=[pltpu.VMEM((HALO + ts, D), F32), pltpu.VMEM((ts, D), F32)],
        compiler_params=_params(("parallel", "arbitrary")),
        name="conv_mixer",
    )(h, g, w_in, b_in, w_dw, b_dw, ln_g, ln_b, w_out, b_out)


def _mlp_kernel(h_ref, g_ref, w1_ref, w2_ref, gf_ref, o_ref, xn_ref, *, final_norm):
    f = pl.program_id(1)

    @pl.when(f == 0)
    def _():
        h = h_ref[...]
        xn_ref[...] = _rmsnorm_bf16(h, g_ref[...])
        o_ref[...] = h

    a = jnp.maximum(jnp.dot(xn_ref[...], w1_ref[...], preferred_element_type=F32), 0.0)
    o_ref[...] += jnp.dot((a * a).astype(BF16), w2_ref[...], preferred_element_type=F32)

    if final_norm:
        @pl.when(f == pl.num_programs(1) - 1)
        def _():
            h = o_ref[...]
            ms = jnp.mean(h * h, axis=-1, keepdims=True)
            o_ref[...] = h * lax.rsqrt(ms + NORM_EPS) * gf_ref[...]


def _mlp(h, g, w1, w2, gf, *, layer, final_norm, tm=512, tf=1024):
    T, D = h.shape
    return pl.pallas_call(
        functools.partial(_mlp_kernel, final_norm=final_norm),
        grid=(T // tm, D_FF // tf),
        in_specs=[
            pl.BlockSpec((tm, D), lambda i, f: (i, 0)),
            pl.BlockSpec((1, D), lambda i, f: (0, 0)),
            pl.BlockSpec((None, D, tf), lambda i, f: (layer, 0, f)),
            pl.BlockSpec((None, tf, D), lambda i, f: (layer, f, 0)),
            pl.BlockSpec((1, D), lambda i, f: (0, 0)),
        ],
        out_specs=pl.BlockSpec((tm, D), lambda i, f: (i, 0)),
        out_shape=jax.ShapeDtypeStruct((T, D), F32),
        scratch_shapes=[pltpu.VMEM((tm, D), BF16)],
        compiler_params=_params(("parallel", "arbitrary")),
        name="mlp",
    )(h, g, w1, w2, gf)


def _rope_tile(x, cos_t, sin_t):
    lane = lax.broadcasted_iota(jnp.int32, (x.shape[0], HEAD_DIM), 1)
    first_half = lane < ROT_DIM // 2
    outs = []
    for hh in range(x.shape[1] // HEAD_DIM):
        xh = x[:, hh * HEAD_DIM:(hh + 1) * HEAD_DIM]
        partner = jnp.where(first_half,
                            pltpu.roll(xh, HEAD_DIM - ROT_DIM // 2, 1),
                            pltpu.roll(xh, ROT_DIM // 2, 1))
        outs.append(xh * cos_t + partner * sin_t)
    return jnp.concatenate(outs, axis=1)


def _qkv_kernel(x_ref, gq_ref, gkv_ref, wq_ref, wkv_ref, cos_ref, sin_ref, q_ref, kv_ref,
                *, q_scale, sub, tn):
    tm = x_ref.shape[0]
    for r in range(tm // sub):
        rs = slice(r * sub, (r + 1) * sub)
        x = x_ref[rs, :]
        xs = x * lax.rsqrt(jnp.mean(x * x, axis=-1, keepdims=True) + NORM_EPS)
        xq = (xs * gq_ref[...]).astype(BF16)
        xkv = (xs * gkv_ref[...]).astype(BF16)
        cos_t, sin_t = cos_ref[rs, :], sin_ref[rs, :]
        for c in range(q_ref.shape[1] // tn):
            cs = slice(c * tn, (c + 1) * tn)
            acc = jnp.dot(xq, wq_ref[:, cs], preferred_element_type=F32)
            q_ref[rs, cs] = _rope_tile(acc, cos_t, sin_t) * q_scale
        k = jnp.dot(xkv, wkv_ref[:, :KV_DIM], preferred_element_type=F32)
        kv_ref[rs, :KV_DIM] = _rope_tile(k, cos_t, sin_t)
        kv_ref[rs, KV_DIM:] = jnp.dot(xkv, wkv_ref[:, KV_DIM:], preferred_element_type=F32)


def _qkv_proj(x, g_q, g_kv, w_q, w_kv, cos_t, sin_t, *, seq, q_scale, tm=512, sub=256, tn=512):
    T, D = x.shape
    qd, kvd = w_q.shape[1], w_kv.shape[1]
    spb = seq // tm
    row = lambda i: (i, 0)
    const = lambda i: (0, 0)
    table = pl.BlockSpec((tm, HEAD_DIM), lambda i: (i % spb, 0))
    return pl.pallas_call(
        functools.partial(_qkv_kernel, q_scale=q_scale, sub=sub, tn=tn),
        grid=(T // tm,),
        in_specs=[
            pl.BlockSpec((tm, D), row),
            pl.BlockSpec((1, D), const),
            pl.BlockSpec((1, D), const),
            pl.BlockSpec((D, qd), const),
            pl.BlockSpec((D, kvd), const),
            table, table,
        ],
        out_specs=[pl.BlockSpec((tm, qd), row), pl.BlockSpec((tm, kvd), row)],
        out_shape=[jax.ShapeDtypeStruct((T, qd), F32), jax.ShapeDtypeStruct((T, kvd), F32)],
        compiler_params=_params(("parallel",)),
        name="qkv_proj",
    )(x, g_q, g_kv, w_q, w_kv, cos_t, sin_t)


def _attn_kernel(q_ref, k_ref, v_ref, o_ref, acc_ref, m_ref, l_ref, eye_ref, bias_ref, *, seq):
    rpc = seq // MAX_DIL
    gq = GROUP * BLOCK

    def mask_t(nk, q_major_shift, k_major_shift, minor_scale, base):
        j = lax.broadcasted_iota(jnp.int32, (nk, BLOCK), 0)
        e = lax.broadcasted_iota(jnp.int32, (nk, BLOCK), 1)
        qmaj, qmin = e >> q_major_shift, e & ((1 << q_major_shift) - 1)
        kmaj, kmin = j >> k_major_shift, j & ((1 << k_major_shift) - 1)
        dist = base + minor_scale * (qmin - kmin) + (qmaj - kmaj)
        return jnp.where(jnp.logical_and(dist >= 0, dist <= BLOCK), 0.0, MASK_VALUE).astype(BF16)

    per1 = BLOCK // MAX_DIL
    dil = DILATIONS[1]
    sub = MAX_DIL // dil
    per4 = BLOCK // sub
    bias_ref[0] = mask_t(2 * BLOCK, 3, 4, MAX_DIL, BLOCK)
    bias_ref[1] = mask_t(2 * BLOCK, 3, 4, MAX_DIL, 0)
    bias_ref[2] = mask_t(2 * BLOCK, 5, 6, sub, BLOCK)
    bias_ref[3] = mask_t(2 * BLOCK, 5, 6, sub, 0)
    bias_ref[4, 0:BLOCK, :] = mask_t(BLOCK, 7, 7, 1, 0)
    row = lax.broadcasted_iota(jnp.int32, (gq, BLOCK), 0) & (BLOCK - 1)
    col = lax.broadcasted_iota(jnp.int32, (gq, BLOCK), 1)
    eye_ref[...] = jnp.where(row == col, 1.0, 0.0).astype(BF16)

    def gather(ref, chunks, lanes):
        return jnp.concatenate([ref[0, pl.ds(st, sz), lanes] for st, sz in chunks], axis=0)

    def block(q_chunks, kv_chunks, bias_t, first, last):
        nk = bias_t.shape[0]
        kk = gather(k_ref, kv_chunks, slice(None)).astype(BF16)
        vv = gather(v_ref, kv_chunks, slice(None)).astype(BF16)
        qh = jnp.concatenate(
            [gather(q_ref, q_chunks, slice(g * HEAD_DIM, (g + 1) * HEAD_DIM)) for g in range(GROUP)],
            axis=0).astype(BF16)
        q_aug = jnp.concatenate([qh, eye_ref[...]], axis=1)
        k_aug = jnp.concatenate([kk, bias_t], axis=1)
        v_aug = jnp.concatenate([vv, jnp.ones((nk, LANES), BF16)], axis=1)
        s = lax.dot_general(q_aug, k_aug, (((1,), (1,)), ((), ())), preferred_element_type=F32)
        m_b = jnp.broadcast_to(jnp.max(s, axis=-1, keepdims=True), (gq, LANES))

        def state(ref):
            return jnp.concatenate([gather(ref.at[g:g + 1], q_chunks, slice(None))
                                    for g in range(GROUP)], axis=0)

        if first:
            m_new = m_b
        else:
            m_old = state(m_ref)
            m_new = jnp.maximum(m_old, m_b)
        p = jnp.exp2(s - jnp.concatenate([m_new] * (nk // LANES), axis=1))
        o_aug = jnp.dot(p.astype(BF16), v_aug, preferred_element_type=F32)
        o_b, l_b = o_aug[:, :HEAD_DIM], o_aug[:, HEAD_DIM:]
        if first:
            acc, l = o_b, l_b
        else:
            alpha = jnp.exp2(m_old - m_new)
            acc = alpha * state(acc_ref) + o_b
            l = alpha * state(l_ref) + l_b

        if last:
            out = (acc / l).astype(o_ref.dtype)
            for g in range(GROUP):
                off = g * BLOCK
                for st, sz in q_chunks:
                    o_ref[0, pl.ds(st, sz), g * HEAD_DIM:(g + 1) * HEAD_DIM] = out[off:off + sz]
                    off += sz
        else:
            for g in range(GROUP):
                off = g * BLOCK
                for st, sz in q_chunks:
                    acc_ref[g, pl.ds(st, sz), :] = acc[off:off + sz]
                    m_ref[g, pl.ds(st, sz), :] = m_new[off:off + sz]
                    l_ref[g, pl.ds(st, sz), :] = l[off:off + sz]
                    off += sz

    def body1(nb, carry):
        q0 = pl.multiple_of(nb * per1, per1)
        k0 = pl.multiple_of(jnp.maximum(nb - 1, 0) * per1, per1)
        block([(c * rpc + q0, per1) for c in range(MAX_DIL)],
              [(c * rpc + k0, 2 * per1) for c in range(MAX_DIL)],
              bias_ref[jnp.where(nb > 0, 0, 1)], first=True, last=False)
        return carry

    lax.fori_loop(0, seq // BLOCK, body1, 0, unroll=ATTN_UNROLL)

    nblk4 = seq // (dil * BLOCK)

    def body4(idx, carry):
        r4, n = idx // nblk4, idx % nblk4
        q0 = pl.multiple_of(n * per4, per4)
        k0 = pl.multiple_of(jnp.maximum(n - 1, 0) * per4, per4)
        block([((r4 + dil * c) * rpc + q0, per4) for c in range(sub)],
              [((r4 + dil * c) * rpc + k0, 2 * per4) for c in range(sub)],
              bias_ref[jnp.where(n > 0, 2, 3)], first=False, last=False)
        return carry

    lax.fori_loop(0, dil * nblk4, body4, 0, unroll=ATTN_UNROLL)

    def body16(c, carry):
        r0 = pl.multiple_of(c * rpc, rpc)
        block([(r0, BLOCK)], [(r0, BLOCK)], bias_ref[4, 0:BLOCK, :], first=False, last=True)
        return carry

    lax.fori_loop(0, MAX_DIL, body16, 0, unroll=ATTN_UNROLL)


def _attention(q, kv, *, batch, seq):
    assert DILATIONS == (1, 4, 16) and all(w // d == BLOCK for w, d in zip(WINDOWS, DILATIONS))
    assert seq == MAX_DIL * BLOCK
    qd = N_HEADS * HEAD_DIM
    q3 = q.reshape(batch, seq, qd)
    kv3 = kv.reshape(batch, seq, 2 * KV_DIM)
    gw = GROUP * HEAD_DIM
    o = pl.pallas_call(
        functools.partial(_attn_kernel, seq=seq),
        grid=(batch, N_KV_HEADS),
        in_specs=[
            pl.BlockSpec((1, seq, gw), lambda b, h: (b, 0, h)),
            pl.BlockSpec((1, seq, HEAD_DIM), lambda b, h: (b, 0, h)),
            pl.BlockSpec((1, seq, HEAD_DIM), lambda b, h: (b, 0, N_KV_HEADS + h)),
        ],
        out_specs=pl.BlockSpec((1, seq, gw), lambda b, h: (b, 0, h)),
        out_shape=jax.ShapeDtypeStruct((batch, seq, qd), BF16),
        scratch_shapes=[pltpu.VMEM((GROUP, seq, LANES), F32)] * 3
        + [pltpu.VMEM((GROUP * BLOCK, BLOCK), BF16), pltpu.VMEM((5, 2 * BLOCK, BLOCK), BF16)],
        compiler_params=_params(("parallel", "parallel")),
        name="attn",
    )(q3, kv3, kv3)
    return o.reshape(batch * seq, qd)


def _out_proj_kernel(o_ref, w_ref, h_ref, out_ref):
    out_ref[...] = h_ref[...] + jnp.dot(o_ref[...], w_ref[...], preferred_element_type=F32)


def _out_proj(o, w, h, *, tm=512):
    T, D = h.shape
    qd = N_HEADS * HEAD_DIM
    row = lambda i: (i, 0)
    return pl.pallas_call(
        _out_proj_kernel,
        grid=(T // tm,),
        in_specs=[pl.BlockSpec((tm, qd), row), pl.BlockSpec((qd, D), lambda i: (0, 0)),
                  pl.BlockSpec((tm, D), row)],
        out_specs=pl.BlockSpec((tm, D), row),
        out_shape=jax.ShapeDtypeStruct((T, D), F32),
        compiler_params=_params(("parallel",)),
        name="out_proj",
    )(o, w, h)


def _rope_tables(pos):
    seq = pos.shape[0]
    inv = ROPE_THETA ** (-jnp.arange(0, ROT_DIM, 2, dtype=F32) / ROT_DIM)
    ang = pos[:, None] * inv[None, :]
    cos, sin = jnp.cos(ang), jnp.sin(ang)
    cos_t = jnp.concatenate([cos, cos, jnp.ones((seq, HEAD_DIM - ROT_DIM), F32)], axis=1)
    sin_t = jnp.concatenate([-sin, sin, jnp.zeros((seq, HEAD_DIM - ROT_DIM), F32)], axis=1)
    return cos_t, sin_t


def kernel(x, norm_mix, norm_mlp, conv_w_in, conv_b_in, conv_w_dw, conv_b_dw, conv_ln_g, conv_ln_b,
           conv_w_out, conv_b_out, kv_norm, w_kv, attn_w_q, attn_w_o, mlp_w_in, mlp_w_out, final_norm):
    B, S, D = x.shape
    T = B * S
    assert D == D_MODEL
    row = lambda v: v.reshape(1, -1)
    w_mlp_in = mlp_w_in.astype(BF16)
    w_mlp_out = mlp_w_out.astype(BF16)
    rows = jnp.arange(S)
    pos = MAX_DIL * (rows % (S // MAX_DIL)) + rows // (S // MAX_DIL)
    cos_t, sin_t = _rope_tables(pos.astype(F32))

    h = x.reshape(T, D)

    h = _conv_mixer(h, row(norm_mix[0]), conv_w_in[0].astype(BF16), row(conv_b_in[0]), conv_w_dw[0],
                    row(conv_b_dw[0]), row(conv_ln_g[0]), row(conv_ln_b[0]),
                    conv_w_out[0].astype(BF16), row(conv_b_out[0]), batch=B, seq=S)
    h = _mlp(h, row(norm_mlp[0]), w_mlp_in, w_mlp_out, row(final_norm), layer=0, final_norm=False)

    h = h.reshape(B, S // MAX_DIL, MAX_DIL, D).transpose(0, 2, 1, 3).reshape(T, D)

    q, kv = _qkv_proj(h, row(norm_mix[1]), row(kv_norm), attn_w_q[0].astype(BF16), w_kv.astype(BF16),
                      cos_t, sin_t, seq=S, q_scale=LOG2_E / math.sqrt(HEAD_DIM))

    o = _attention(q, kv, batch=B, seq=S)
    h = _out_proj(o, attn_w_o[0].astype(BF16), h)
    h = _mlp(h, row(norm_mlp[1]), w_mlp_in, w_mlp_out, row(final_norm), layer=1, final_norm=True)
    return h.reshape(B, MAX_DIL, S // MAX_DIL, D).transpose(0, 2, 1, 3).reshape(B, S, D)
```

```python
import functools
import math

import jax
import jax.numpy as jnp
from jax import lax
from jax.experimental import pallas as pl
from jax.experimental.pallas import tpu as pltpu

F32 = jnp.float32
BF16 = jnp.bfloat16

D_MODEL = 2048
CONV_WIDTH = 31
HEAD_DIM = 128
N_HEADS = 16
N_KV_HEADS = 4
GROUP = N_HEADS // N_KV_HEADS
KV_DIM = N_KV_HEADS * HEAD_DIM
WINDOWS = (128, 512, 2048)
DILATIONS = (1, 4, 16)
BLOCK = 128
ROPE_THETA = 500000.0
ROT_DIM = HEAD_DIM // 4
D_FF = 4 * D_MODEL
NORM_EPS = 1e-6
LN_EPS = 1e-5

LANES = 128
SUBLANES = 8
HALO = 32
VMEM_LIMIT = 56 * 1024 * 1024
MASK_VALUE = -1e30
LOG2_E = math.log2(math.e)
MAX_DIL = max(DILATIONS)
ATTN_UNROLL = 16


def _params(sem):
    return pltpu.CompilerParams(dimension_semantics=sem, vmem_limit_bytes=VMEM_LIMIT)


def _rmsnorm_bf16(x, g):
    ms = jnp.mean(x * x, axis=-1, keepdims=True)
    return (x * lax.rsqrt(ms + NORM_EPS) * g).astype(BF16)


def _conv_mixer_kernel(x_ref, g_ref, win_ref, bin_ref, wdw_ref, bdw_ref, lng_ref, lnb_ref, wo_ref, bo_ref,
                       o_ref, ext_ref, cv_ref, *, ts, tn):
    i = pl.program_id(1)

    @pl.when(i == 0)
    def _():
        ext_ref[0:HALO, :] = jnp.zeros((HALO, D_MODEL), F32)

    @pl.when(i > 0)
    def _():
        ext_ref[0:HALO, :] = ext_ref[ts:ts + HALO, :]

    x = x_ref[...]
    xn = _rmsnorm_bf16(x, g_ref[...])

    for c in range(D_MODEL // tn):
        cs = slice(c * tn, (c + 1) * tn)
        gs = slice(D_MODEL + c * tn, D_MODEL + (c + 1) * tn)
        a = jnp.dot(xn, win_ref[:, cs], preferred_element_type=F32) + bin_ref[:, cs]
        gate = jnp.dot(xn, win_ref[:, gs], preferred_element_type=F32) + bin_ref[:, gs]
        ext_ref[HALO:HALO + ts, cs] = a * jax.nn.sigmoid(gate)

        for lt in range(c * tn // LANES, (c + 1) * tn // LANES):
            ls = slice(lt * LANES, (lt + 1) * LANES)
            acc = None
            for rho in range(SUBLANES):
                z = None
                for d in range(rho, CONV_WIDTH, SUBLANES):
                    start = HALO - SUBLANES - (d - rho)
                    k = CONV_WIDTH - 1 - d
                    term = ext_ref[start:start + ts + SUBLANES, ls] * wdw_ref[k:k + 1, ls]
                    z = term if z is None else z + term
                part = z[SUBLANES - rho:SUBLANES - rho + ts]
                acc = part if acc is None else acc + part
            cv_ref[:, ls] = acc + bdw_ref[:, ls]

    cv = cv_ref[...]
    mu = jnp.mean(cv, axis=-1, keepdims=True)
    xc = cv - mu
    var = jnp.mean(xc * xc, axis=-1, keepdims=True)
    y = xc * lax.rsqrt(var + LN_EPS) * lng_ref[...] + lnb_ref[...]
    y = (y * jax.nn.sigmoid(y)).astype(BF16)
    o_ref[...] = x + bo_ref[...] + jnp.dot(y, wo_ref[...], preferred_element_type=F32)


def _conv_mixer(h, g, w_in, b_in, w_dw, b_dw, ln_g, ln_b, w_out, b_out, *, batch, seq, ts=256, tn=256):
    D = D_MODEL
    n = seq // ts
    row = lambda b, i: (b * n + i, 0)
    const = lambda b, i: (0, 0)
    resident = lambda shape: pl.BlockSpec(shape, const, pipeline_mode=pl.Buffered(1))
    return pl.pallas_call(
        functools.partial(_conv_mixer_kernel, ts=ts, tn=tn),
        grid=(batch, n),
        in_specs=[
            pl.BlockSpec((ts, D), row),
            resident((1, D)),
            resident((D, 2 * D)),
            resident((1, 2 * D)),
            resident((CONV_WIDTH, D)),
            resident((1, D)),
            resident((1, D)),
            resident((1, D)),
            resident((D, D)),
            resident((1, D)),
        ],
        out_specs=pl.BlockSpec((ts, D), row),
        out_shape=jax.ShapeDtypeStruct((batch * seq, D), F32),
        scratch_shapes=[pltpu.VMEM((HALO + ts, D), F32), pltpu.VMEM((ts, D), F32)],
        compiler_params=_params(("parallel", "arbitrary")),
        name="conv_mixer",
    )(h, g, w_in, b_in, w_dw, b_dw, ln_g, ln_b, w_out, b_out)


def _mlp_kernel(h_ref, g_ref, w1_ref, w2_ref, gf_ref, o_ref, xn_ref, *, final_norm):
    f = pl.program_id(1)

    @pl.when(f == 0)
    def _():
        h = h_ref[...]
        xn_ref[...] = _rmsnorm_bf16(h, g_ref[...])
        o_ref[...] = h

    a = jnp.maximum(jnp.dot(xn_ref[...], w1_ref[...], preferred_element_type=F32), 0.0)
    o_ref[...] += jnp.dot((a * a).astype(BF16), w2_ref[...], preferred_element_type=F32)

    if final_norm:
        @pl.when(f == pl.num_programs(1) - 1)
        def _():
            h = o_ref[...]
            ms = jnp.mean(h * h, axis=-1, keepdims=True)
            o_ref[...] = h * lax.rsqrt(ms + NORM_EPS) * gf_ref[...]


def _mlp(h, g, w1, w2, gf, *, layer, final_norm, tm=512, tf=1024):
    T, D = h.shape
    return pl.pallas_call(
        functools.partial(_mlp_kernel, final_norm=final_norm),
        grid=(T // tm, D_FF // tf),
        in_specs=[
            pl.BlockSpec((tm, D), lambda i, f: (i, 0)),
            pl.BlockSpec((1, D), lambda i, f: (0, 0)),
            pl.BlockSpec((None, D, tf), lambda i, f: (layer, 0, f)),
            pl.BlockSpec((None, tf, D), lambda i, f: (layer, f, 0)),
            pl.BlockSpec((1, D), lambda i, f: (0, 0)),
        ],
        out_specs=pl.BlockSpec((tm, D), lambda i, f: (i, 0)),
        out_shape=jax.ShapeDtypeStruct((T, D), F32),
        scratch_shapes=[pltpu.VMEM((tm, D), BF16)],
        compiler_params=_params(("parallel", "arbitrary")),
        name="mlp",
    )(h, g, w1, w2, gf)


def _rope_tile(x, cos_t, sin_t):
    lane = lax.broadcasted_iota(jnp.int32, (x.shape[0], HEAD_DIM), 1)
    first_half = lane < ROT_DIM // 2
    outs = []
    for hh in range(x.shape[1] // HEAD_DIM):
        xh = x[:, hh * HEAD_DIM:(hh + 1) * HEAD_DIM]
        partner = jnp.where(first_half,
                            pltpu.roll(xh, HEAD_DIM - ROT_DIM // 2, 1),
                            pltpu.roll(xh, ROT_DIM // 2, 1))
        outs.append(xh * cos_t + partner * sin_t)
    return jnp.concatenate(outs, axis=1)


def _residue_tile_copies(hbm_ref, buf_ref, sem_ref, tile, slot, *, tm, seq, to_hbm):
    rpc = seq // MAX_DIL
    tiles_per_seq = seq // tm
    b = tile // tiles_per_seq
    c0 = (tile % tiles_per_seq) * (tm // rpc)
    copies = []
    for k in range(tm // rpc):
        hbm_rows = hbm_ref.at[b, :, c0 + k, :]
        vmem_rows = buf_ref.at[slot, pl.ds(k * rpc, rpc), :]
        src, dst = (vmem_rows, hbm_rows) if to_hbm else (hbm_rows, vmem_rows)
        copies.append(pltpu.make_async_copy(src, dst, sem_ref.at[slot]))
    return copies


def _prefetch_residue_tile(hbm_ref, buf_ref, sem_ref, *, tm, seq):
    i = pl.program_id(0)
    slot = i % 2
    fetch = functools.partial(_residue_tile_copies, hbm_ref, buf_ref, sem_ref, tm=tm, seq=seq, to_hbm=False)

    @pl.when(i == 0)
    def _():
        for c in fetch(0, 0):
            c.start()

    @pl.when(i + 1 < pl.num_programs(0))
    def _():
        for c in fetch(i + 1, 1 - slot):
            c.start()

    for c in fetch(i, slot):
        c.wait()
    return slot


def _qkv_kernel(x_hbm, gq_ref, gkv_ref, wq_ref, wkv_ref, cos_ref, sin_ref, q_ref, kv_ref, x_buf, x_sem,
                *, q_scale, sub, tn, seq):
    tm = q_ref.shape[0]
    slot = _prefetch_residue_tile(x_hbm, x_buf, x_sem, tm=tm, seq=seq)
    for r in range(tm // sub):
        rs = slice(r * sub, (r + 1) * sub)
        x = x_buf[slot, rs, :]
        xs = x * lax.rsqrt(jnp.mean(x * x, axis=-1, keepdims=True) + NORM_EPS)
        xq = (xs * gq_ref[...]).astype(BF16)
        xkv = (xs * gkv_ref[...]).astype(BF16)
        cos_t, sin_t = cos_ref[rs, :], sin_ref[rs, :]
        for c in range(q_ref.shape[1] // tn):
            cs = slice(c * tn, (c + 1) * tn)
            acc = jnp.dot(xq, wq_ref[:, cs], preferred_element_type=F32)
            q_ref[rs, cs] = _rope_tile(acc, cos_t, sin_t) * q_scale
        k = jnp.dot(xkv, wkv_ref[:, :KV_DIM], preferred_element_type=F32)
        kv_ref[rs, :KV_DIM] = _rope_tile(k, cos_t, sin_t)
        kv_ref[rs, KV_DIM:] = jnp.dot(xkv, wkv_ref[:, KV_DIM:], preferred_element_type=F32)


def _qkv_proj(x, g_q, g_kv, w_q, w_kv, cos_t, sin_t, *, seq, q_scale, tm=512, sub=256, tn=512):
    D = x.shape[-1]
    T = x.shape[0] * seq
    qd, kvd = w_q.shape[1], w_kv.shape[1]
    spb = seq // tm
    row = lambda i: (i, 0)
    const = lambda i: (0, 0)
    table = pl.BlockSpec((tm, HEAD_DIM), lambda i: (i % spb, 0))
    return pl.pallas_call(
        functools.partial(_qkv_kernel, q_scale=q_scale, sub=sub, tn=tn, seq=seq),
        grid=(T // tm,),
        in_specs=[
            pl.BlockSpec(memory_space=pl.ANY),
            pl.BlockSpec((1, D), const),
            pl.BlockSpec((1, D), const),
            pl.BlockSpec((D, qd), const),
            pl.BlockSpec((D, kvd), const),
            table, table,
        ],
        out_specs=[pl.BlockSpec((tm, qd), row), pl.BlockSpec((tm, kvd), row)],
        out_shape=[jax.ShapeDtypeStruct((T, qd), F32), jax.ShapeDtypeStruct((T, kvd), F32)],
        scratch_shapes=[pltpu.VMEM((2, tm, D), F32), pltpu.SemaphoreType.DMA((2,))],
        compiler_params=_params(("arbitrary",)),
        name="qkv_proj",
    )(x, g_q, g_kv, w_q, w_kv, cos_t, sin_t)


def _attn_kernel(q_ref, k_ref, v_ref, o_ref, acc_ref, m_ref, l_ref, eye_ref, bias_ref, *, seq):
    rpc = seq // MAX_DIL
    gq = GROUP * BLOCK

    def mask_t(nk, q_major_shift, k_major_shift, minor_scale, base):
        j = lax.broadcasted_iota(jnp.int32, (nk, BLOCK), 0)
        e = lax.broadcasted_iota(jnp.int32, (nk, BLOCK), 1)
        qmaj, qmin = e >> q_major_shift, e & ((1 << q_major_shift) - 1)
        kmaj, kmin = j >> k_major_shift, j & ((1 << k_major_shift) - 1)
        dist = base + minor_scale * (qmin - kmin) + (qmaj - kmaj)
        return jnp.where(jnp.logical_and(dist >= 0, dist <= BLOCK), 0.0, MASK_VALUE).astype(BF16)

    per1 = BLOCK // MAX_DIL
    dil = DILATIONS[1]
    sub = MAX_DIL // dil
    per4 = BLOCK // sub
    bias_ref[0] = mask_t(2 * BLOCK, 3, 4, MAX_DIL, BLOCK)
    bias_ref[1] = mask_t(2 * BLOCK, 3, 4, MAX_DIL, 0)
    bias_ref[2] = mask_t(2 * BLOCK, 5, 6, sub, BLOCK)
    bias_ref[3] = mask_t(2 * BLOCK, 5, 6, sub, 0)
    bias_ref[4, 0:BLOCK, :] = mask_t(BLOCK, 7, 7, 1, 0)
    row = lax.broadcasted_iota(jnp.int32, (gq, BLOCK), 0) & (BLOCK - 1)
    col = lax.broadcasted_iota(jnp.int32, (gq, BLOCK), 1)
    eye_ref[...] = jnp.where(row == col, 1.0, 0.0).astype(BF16)

    def gather(ref, chunks, lanes):
        return jnp.concatenate([ref[0, pl.ds(st, sz), lanes] for st, sz in chunks], axis=0)

    def block(q_chunks, kv_chunks, bias_t, first, last):
        nk = bias_t.shape[0]
        kk = gather(k_ref, kv_chunks, slice(None)).astype(BF16)
        vv = gather(v_ref, kv_chunks, slice(None)).astype(BF16)
        qh = jnp.concatenate(
            [gather(q_ref, q_chunks, slice(g * HEAD_DIM, (g + 1) * HEAD_DIM)) for g in range(GROUP)],
            axis=0).astype(BF16)
        q_aug = jnp.concatenate([qh, eye_ref[...]], axis=1)
        k_aug = jnp.concatenate([kk, bias_t], axis=1)
        v_aug = jnp.concatenate([vv, jnp.ones((nk, LANES), BF16)], axis=1)
        s = lax.dot_general(q_aug, k_aug, (((1,), (1,)), ((), ())), preferred_element_type=F32)
        m_b = jnp.broadcast_to(jnp.max(s, axis=-1, keepdims=True), (gq, LANES))

        def state(ref):
            return jnp.concatenate([gather(ref.at[g:g + 1], q_chunks, slice(None))
                                    for g in range(GROUP)], axis=0)

        if first:
            m_new = m_b
        else:
            m_old = state(m_ref)
            m_new = jnp.maximum(m_old, m_b)
        p = jnp.exp2(s - jnp.concatenate([m_new] * (nk // LANES), axis=1))
        o_aug = jnp.dot(p.astype(BF16), v_aug, preferred_element_type=F32)
        o_b, l_b = o_aug[:, :HEAD_DIM], o_aug[:, HEAD_DIM:]
        if first:
            acc, l = o_b, l_b
        else:
            alpha = jnp.exp2(m_old - m_new)
            acc = alpha * state(acc_ref) + o_b
            l = alpha * state(l_ref) + l_b

        if last:
            out = (acc / l).astype(o_ref.dtype)
            for g in range(GROUP):
                off = g * BLOCK
                for st, sz in q_chunks:
                    o_ref[0, pl.ds(st, sz), g * HEAD_DIM:(g + 1) * HEAD_DIM] = out[off:off + sz]
                    off += sz
        else:
            for g in range(GROUP):
                off = g * BLOCK
                for st, sz in q_chunks:
                    acc_ref[g, pl.ds(st, sz), :] = acc[off:off + sz]
                    m_ref[g, pl.ds(st, sz), :] = m_new[off:off + sz]
                    l_ref[g, pl.ds(st, sz), :] = l[off:off + sz]
                    off += sz

    def body1(nb, carry):
        q0 = pl.multiple_of(nb * per1, per1)
        k0 = pl.multiple_of(jnp.maximum(nb - 1, 0) * per1, per1)
        block([(c * rpc + q0, per1) for c in range(MAX_DIL)],
              [(c * rpc + k0, 2 * per1) for c in range(MAX_DIL)],
              bias_ref[jnp.where(nb > 0, 0, 1)], first=True, last=False)
        return carry

    lax.fori_loop(0, seq // BLOCK, body1, 0, unroll=ATTN_UNROLL)

    nblk4 = seq // (dil * BLOCK)

    def body4(idx, carry):
        r4, n = idx // nblk4, idx % nblk4
        q0 = pl.multiple_of(n * per4, per4)
        k0 = pl.multiple_of(jnp.maximum(n - 1, 0) * per4, per4)
        block([((r4 + dil * c) * rpc + q0, per4) for c in range(sub)],
              [((r4 + dil * c) * rpc + k0, 2 * per4) for c in range(sub)],
              bias_ref[jnp.where(n > 0, 2, 3)], first=False, last=False)
        return carry

    lax.fori_loop(0, dil * nblk4, body4, 0, unroll=ATTN_UNROLL)

    def body16(c, carry):
        r0 = pl.multiple_of(c * rpc, rpc)
        block([(r0, BLOCK)], [(r0, BLOCK)], bias_ref[4, 0:BLOCK, :], first=False, last=True)
        return carry

    lax.fori_loop(0, MAX_DIL, body16, 0, unroll=ATTN_UNROLL)


def _attention(q, kv, *, batch, seq):
    assert DILATIONS == (1, 4, 16) and all(w // d == BLOCK for w, d in zip(WINDOWS, DILATIONS))
    assert seq == MAX_DIL * BLOCK
    qd = N_HEADS * HEAD_DIM
    q3 = q.reshape(batch, seq, qd)
    kv3 = kv.reshape(batch, seq, 2 * KV_DIM)
    gw = GROUP * HEAD_DIM
    o = pl.pallas_call(
        functools.partial(_attn_kernel, seq=seq),
        grid=(batch, N_KV_HEADS),
        in_specs=[
            pl.BlockSpec((1, seq, gw), lambda b, h: (b, 0, h)),
            pl.BlockSpec((1, seq, HEAD_DIM), lambda b, h: (b, 0, h)),
            pl.BlockSpec((1, seq, HEAD_DIM), lambda b, h: (b, 0, N_KV_HEADS + h)),
        ],
        out_specs=pl.BlockSpec((1, seq, gw), lambda b, h: (b, 0, h)),
        out_shape=jax.ShapeDtypeStruct((batch, seq, qd), BF16),
        scratch_shapes=[pltpu.VMEM((GROUP, seq, LANES), F32)] * 3
        + [pltpu.VMEM((GROUP * BLOCK, BLOCK), BF16), pltpu.VMEM((5, 2 * BLOCK, BLOCK), BF16)],
        compiler_params=_params(("parallel", "parallel")),
        name="attn",
    )(q3, kv3, kv3)
    return o.reshape(batch * seq, qd)


def _out_proj_kernel(o_ref, w_ref, h_hbm, out_hbm, h_buf, out_buf, h_sem, out_sem, *, seq):
    tm = o_ref.shape[0]
    i = pl.program_id(0)
    last = pl.num_programs(0) - 1
    slot = _prefetch_residue_tile(h_hbm, h_buf, h_sem, tm=tm, seq=seq)
    write = functools.partial(_residue_tile_copies, out_hbm, out_buf, out_sem, tm=tm, seq=seq, to_hbm=True)

    @pl.when(i >= 2)
    def _():
        for c in write(i - 2, slot):
            c.wait()

    out_buf[slot] = h_buf[slot] + jnp.dot(o_ref[...], w_ref[...], preferred_element_type=F32)
    for c in write(i, slot):
        c.start()

    @pl.when(i == last)
    def _():
        for c in write(i - 1, 1 - slot) + write(i, slot):
            c.wait()


def _out_proj(o, w, h, *, seq, tm=512):
    D = h.shape[-1]
    T, qd = o.shape
    assert T // tm >= 2
    row = lambda i: (i, 0)
    return pl.pallas_call(
        functools.partial(_out_proj_kernel, seq=seq),
        grid=(T // tm,),
        in_specs=[pl.BlockSpec((tm, qd), row), pl.BlockSpec((qd, D), lambda i: (0, 0)),
                  pl.BlockSpec(memory_space=pl.ANY)],
        out_specs=pl.BlockSpec(memory_space=pl.ANY),
        out_shape=jax.ShapeDtypeStruct(h.shape, F32),
        scratch_shapes=[pltpu.VMEM((2, tm, D), F32), pltpu.VMEM((2, tm, D), F32),
                        pltpu.SemaphoreType.DMA((2,)), pltpu.SemaphoreType.DMA((2,))],
        compiler_params=_params(("arbitrary",)),
        name="out_proj",
    )(o, w, h)


def _rope_tables(pos):
    seq = pos.shape[0]
    inv = ROPE_THETA ** (-jnp.arange(0, ROT_DIM, 2, dtype=F32) / ROT_DIM)
    ang = pos[:, None] * inv[None, :]
    cos, sin = jnp.cos(ang), jnp.sin(ang)
    cos_t = jnp.concatenate([cos, cos, jnp.ones((seq, HEAD_DIM - ROT_DIM), F32)], axis=1)
    sin_t = jnp.concatenate([-sin, sin, jnp.zeros((seq, HEAD_DIM - ROT_DIM), F32)], axis=1)
    return cos_t, sin_t


def kernel(x, norm_mix, norm_mlp, conv_w_in, conv_b_in, conv_w_dw, conv_b_dw, conv_ln_g, conv_ln_b,
           conv_w_out, conv_b_out, kv_norm, w_kv, attn_w_q, attn_w_o, mlp_w_in, mlp_w_out, final_norm):
    B, S, D = x.shape
    T = B * S
    assert D == D_MODEL
    row = lambda v: v.reshape(1, -1)
    w_mlp_in = mlp_w_in.astype(BF16)
    w_mlp_out = mlp_w_out.astype(BF16)
    rows = jnp.arange(S)
    pos = MAX_DIL * (rows % (S // MAX_DIL)) + rows // (S // MAX_DIL)
    cos_t, sin_t = _rope_tables(pos.astype(F32))

    h = x.reshape(T, D)

    h = _conv_mixer(h, row(norm_mix[0]), conv_w_in[0].astype(BF16), row(conv_b_in[0]), conv_w_dw[0],
                    row(conv_b_dw[0]), row(conv_ln_g[0]), row(conv_ln_b[0]),
                    conv_w_out[0].astype(BF16), row(conv_b_out[0]), batch=B, seq=S)
    h = _mlp(h, row(norm_mlp[0]), w_mlp_in, w_mlp_out, row(final_norm), layer=0, final_norm=False)

    h4 = h.reshape(B, S // MAX_DIL, MAX_DIL, D)

    q, kv = _qkv_proj(h4, row(norm_mix[1]), row(kv_norm), attn_w_q[0].astype(BF16), w_kv.astype(BF16),
                      cos_t, sin_t, seq=S, q_scale=LOG2_E / math.sqrt(HEAD_DIM))

    o = _attention(q, kv, batch=B, seq=S)
    h = _out_proj(o, attn_w_o[0].astype(BF16), h4, seq=S).reshape(T, D)
    h = _mlp(h, row(norm_mlp[1]), w_mlp_in, w_mlp_out, row(final_norm), layer=1, final_norm=True)
    return h.reshape(B, S, D)
```

```python
import functools
import math

import jax
import jax.numpy as jnp
from jax import lax
from jax.experimental import pallas as pl
from jax.experimental.pallas import tpu as pltpu

F32 = jnp.float32
BF16 = jnp.bfloat16

D_MODEL = 2048
CONV_WIDTH = 31
HEAD_DIM = 128
N_HEADS = 16
N_KV_HEADS = 4
GROUP = N_HEADS // N_KV_HEADS
KV_DIM = N_KV_HEADS * HEAD_DIM
WINDOWS = (128, 512, 2048)
DILATIONS = (1, 4, 16)
BLOCK = 128
ROPE_THETA = 500000.0
ROT_DIM = HEAD_DIM // 4
D_FF = 4 * D_MODEL
NORM_EPS = 1e-6
LN_EPS = 1e-5

LANES = 128
SUBLANES = 8
HALO = 32
VMEM_LIMIT = 56 * 1024 * 1024
MASK_VALUE = -1e30
LOG2_E = math.log2(math.e)
MAX_DIL = max(DILATIONS)
ATTN_UNROLL = 16


def _params(sem):
    return pltpu.CompilerParams(dimension_semantics=sem, vmem_limit_bytes=VMEM_LIMIT)


def _rmsnorm_bf16(x, g):
    ms = jnp.mean(x * x, axis=-1, keepdims=True)
    return (x * lax.rsqrt(ms + NORM_EPS) * g).astype(BF16)


def _conv_mixer_kernel(x_ref, g_ref, win_ref, bin_ref, wdw_ref, bdw_ref, lng_ref, lnb_ref, wo_ref, bo_ref,
                       o_ref, ext_ref, cv_ref, *, ts, tn):
    i = pl.program_id(1)

    @pl.when(i == 0)
    def _():
        ext_ref[0:HALO, :] = jnp.zeros((HALO, D_MODEL), F32)

    @pl.when(i > 0)
    def _():
        ext_ref[0:HALO, :] = ext_ref[ts:ts + HALO, :]

    x = x_ref[...]
    xn = _rmsnorm_bf16(x, g_ref[...])

    for c in range(D_MODEL // tn):
        cs = slice(c * tn, (c + 1) * tn)
        gs = slice(D_MODEL + c * tn, D_MODEL + (c + 1) * tn)
        a = jnp.dot(xn, win_ref[:, cs], preferred_element_type=F32) + bin_ref[:, cs]
        gate = jnp.dot(xn, win_ref[:, gs], preferred_element_type=F32) + bin_ref[:, gs]
        ext_ref[HALO:HALO + ts, cs] = a * jax.nn.sigmoid(gate)

        for lt in range(c * tn // LANES, (c + 1) * tn // LANES):
            ls = slice(lt * LANES, (lt + 1) * LANES)
            acc = None
            for rho in range(SUBLANES):
                z = None
                for d in range(rho, CONV_WIDTH, SUBLANES):
                    start = HALO - SUBLANES - (d - rho)
                    k = CONV_WIDTH - 1 - d
                    term = ext_ref[start:start + ts + SUBLANES, ls] * wdw_ref[k:k + 1, ls]
                    z = term if z is None else z + term
                part = z[SUBLANES - rho:SUBLANES - rho + ts]
                acc = part if acc is None else acc + part
            cv_ref[:, ls] = acc + bdw_ref[:, ls]

    cv = cv_ref[...]
    mu = jnp.mean(cv, axis=-1, keepdims=True)
    xc = cv - mu
    var = jnp.mean(xc * xc, axis=-1, keepdims=True)
    y = xc * lax.rsqrt(var + LN_EPS) * lng_ref[...] + lnb_ref[...]
    y = (y * jax.nn.sigmoid(y)).astype(BF16)
    o_ref[...] = x + bo_ref[...] + jnp.dot(y, wo_ref[...], preferred_element_type=F32)


def _conv_mixer(h, g, w_in, b_in, w_dw, b_dw, ln_g, ln_b, w_out, b_out, *, batch, seq, ts=256, tn=256):
    D = D_MODEL
    n = seq // ts
    row = lambda b, i: (b * n + i, 0)
    const = lambda b, i: (0, 0)
    resident = lambda shape: pl.BlockSpec(shape, const, pipeline_mode=pl.Buffered(1))
    return pl.pallas_call(
        functools.partial(_conv_mixer_kernel, ts=ts, tn=tn),
        grid=(batch, n),
        in_specs=[
            pl.BlockSpec((ts, D), row),
            resident((1, D)),
            resident((D, 2 * D)),
            resident((1, 2 * D)),
            resident((CONV_WIDTH, D)),
            resident((1, D)),
            resident((1, D)),
            resident((1, D)),
            resident((D, D)),
            resident((1, D)),
        ],
        out_specs=pl.BlockSpec((ts, D), row),
        out_shape=jax.ShapeDtypeStruct((batch * seq, D), F32),
        scratch_shapes=[pltpu.VMEM((HALO + ts, D), F32), pltpu.VMEM((ts, D), F32)],
        compiler_params=_params(("parallel", "arbitrary")),
        name="conv_mixer",
    )(h, g, w_in, b_in, w_dw, b_dw, ln_g, ln_b, w_out, b_out)


def _mlp_kernel(h_ref, g_ref, w1_ref, w2_ref, gf_ref, o_ref, xn_ref, *, final_norm):
    f = pl.program_id(1)

    @pl.when(f == 0)
    def _():
        h = h_ref[...]
        xn_ref[...] = _rmsnorm_bf16(h, g_ref[...])
        o_ref[...] = h

    w1 = w1_ref[...].astype(BF16)
    w2 = w2_ref[...].astype(BF16)
    a = jnp.maximum(jnp.dot(xn_ref[...], w1, preferred_element_type=F32), 0.0)
    o_ref[...] += jnp.dot((a * a).astype(BF16), w2, preferred_element_type=F32)

    if final_norm:
        @pl.when(f == pl.num_programs(1) - 1)
        def _():
            h = o_ref[...]
            ms = jnp.mean(h * h, axis=-1, keepdims=True)
            o_ref[...] = h * lax.rsqrt(ms + NORM_EPS) * gf_ref[...]


def _mlp(h, g, w1, w2, gf, *, layer, final_norm, tm=1024, tf=512):
    T, D = h.shape
    return pl.pallas_call(
        functools.partial(_mlp_kernel, final_norm=final_norm),
        grid=(T // tm, D_FF // tf),
        in_specs=[
            pl.BlockSpec((tm, D), lambda i, f: (i, 0)),
            pl.BlockSpec((1, D), lambda i, f: (0, 0)),
            pl.BlockSpec((None, D, tf), lambda i, f: (layer, 0, f)),
            pl.BlockSpec((None, tf, D), lambda i, f: (layer, f, 0)),
            pl.BlockSpec((1, D), lambda i, f: (0, 0)),
        ],
        out_specs=pl.BlockSpec((tm, D), lambda i, f: (i, 0)),
        out_shape=jax.ShapeDtypeStruct((T, D), F32),
        scratch_shapes=[pltpu.VMEM((tm, D), BF16)],
        compiler_params=_params(("parallel", "arbitrary")),
        name="mlp",
    )(h, g, w1, w2, gf)


def _rope_tile(x, cos_t, sin_t):
    lane = lax.broadcasted_iota(jnp.int32, (x.shape[0], HEAD_DIM), 1)
    first_half = lane < ROT_DIM // 2
    outs = []
    for hh in range(x.shape[1] // HEAD_DIM):
        xh = x[:, hh * HEAD_DIM:(hh + 1) * HEAD_DIM]
        partner = jnp.where(first_half,
                            pltpu.roll(xh, HEAD_DIM - ROT_DIM // 2, 1),
                            pltpu.roll(xh, ROT_DIM // 2, 1))
        outs.append(xh * cos_t + partner * sin_t)
    return jnp.concatenate(outs, axis=1)


def _residue_tile_copies(hbm_ref, buf_ref, sem_ref, tile, slot, *, tm, seq, to_hbm):
    rpc = seq // MAX_DIL
    tiles_per_seq = seq // tm
    b = tile // tiles_per_seq
    c0 = (tile % tiles_per_seq) * (tm // rpc)
    copies = []
    for k in range(tm // rpc):
        hbm_rows = hbm_ref.at[b, :, c0 + k, :]
        vmem_rows = buf_ref.at[slot, pl.ds(k * rpc, rpc), :]
        src, dst = (vmem_rows, hbm_rows) if to_hbm else (hbm_rows, vmem_rows)
        copies.append(pltpu.make_async_copy(src, dst, sem_ref.at[slot]))
    return copies


def _prefetch_residue_tile(hbm_ref, buf_ref, sem_ref, *, tm, seq):
    i = pl.program_id(0)
    slot = i % 2
    fetch = functools.partial(_residue_tile_copies, hbm_ref, buf_ref, sem_ref, tm=tm, seq=seq, to_hbm=False)

    @pl.when(i == 0)
    def _():
        for c in fetch(0, 0):
            c.start()

    @pl.when(i + 1 < pl.num_programs(0))
    def _():
        for c in fetch(i + 1, 1 - slot):
            c.start()

    for c in fetch(i, slot):
        c.wait()
    return slot


def _qkv_kernel(x_hbm, gq_ref, gkv_ref, wq_ref, wkv_ref, cos_ref, sin_ref, q_ref, kv_ref, x_buf, x_sem,
                *, q_scale, sub, tn, seq):
    tm = q_ref.shape[0]
    slot = _prefetch_residue_tile(x_hbm, x_buf, x_sem, tm=tm, seq=seq)
    for r in range(tm // sub):
        rs = slice(r * sub, (r + 1) * sub)
        x = x_buf[slot, rs, :]
        xs = x * lax.rsqrt(jnp.mean(x * x, axis=-1, keepdims=True) + NORM_EPS)
        xq = (xs * gq_ref[...]).astype(BF16)
        xkv = (xs * gkv_ref[...]).astype(BF16)
        cos_t, sin_t = cos_ref[rs, :], sin_ref[rs, :]
        for c in range(q_ref.shape[1] // tn):
            cs = slice(c * tn, (c + 1) * tn)
            acc = jnp.dot(xq, wq_ref[:, cs], preferred_element_type=F32)
            q_ref[rs, cs] = _rope_tile(acc, cos_t, sin_t) * q_scale
        k = jnp.dot(xkv, wkv_ref[:, :KV_DIM], preferred_element_type=F32)
        kv_ref[rs, :KV_DIM] = _rope_tile(k, cos_t, sin_t)
        kv_ref[rs, KV_DIM:] = jnp.dot(xkv, wkv_ref[:, KV_DIM:], preferred_element_type=F32)


def _qkv_proj(x, g_q, g_kv, w_q, w_kv, cos_t, sin_t, *, seq, q_scale, tm=512, sub=256, tn=512):
    D = x.shape[-1]
    T = x.shape[0] * seq
    qd, kvd = w_q.shape[1], w_kv.shape[1]
    spb = seq // tm
    row = lambda i: (i, 0)
    const = lambda i: (0, 0)
    table = pl.BlockSpec((tm, HEAD_DIM), lambda i: (i % spb, 0))
    return pl.pallas_call(
        functools.partial(_qkv_kernel, q_scale=q_scale, sub=sub, tn=tn, seq=seq),
        grid=(T // tm,),
        in_specs=[
            pl.BlockSpec(memory_space=pl.ANY),
            pl.BlockSpec((1, D), const),
            pl.BlockSpec((1, D), const),
            pl.BlockSpec((D, qd), const),
            pl.BlockSpec((D, kvd), const),
            table, table,
        ],
        out_specs=[pl.BlockSpec((tm, qd), row), pl.BlockSpec((tm, kvd), row)],
        out_shape=[jax.ShapeDtypeStruct((T, qd), F32), jax.ShapeDtypeStruct((T, kvd), F32)],
        scratch_shapes=[pltpu.VMEM((2, tm, D), F32), pltpu.SemaphoreType.DMA((2,))],
        compiler_params=_params(("arbitrary",)),
        name="qkv_proj",
    )(x, g_q, g_kv, w_q, w_kv, cos_t, sin_t)


def _attn_kernel(q_ref, k_ref, v_ref, o_ref, acc_ref, m_ref, l_ref, eye_ref, bias_ref, *, seq):
    rpc = seq // MAX_DIL
    gq = GROUP * BLOCK

    def mask_t(nk, q_major_shift, k_major_shift, minor_scale, base):
        j = lax.broadcasted_iota(jnp.int32, (nk, BLOCK), 0)
        e = lax.broadcasted_iota(jnp.int32, (nk, BLOCK), 1)
        qmaj, qmin = e >> q_major_shift, e & ((1 << q_major_shift) - 1)
        kmaj, kmin = j >> k_major_shift, j & ((1 << k_major_shift) - 1)
        dist = base + minor_scale * (qmin - kmin) + (qmaj - kmaj)
        return jnp.where(jnp.logical_and(dist >= 0, dist <= BLOCK), 0.0, MASK_VALUE).astype(BF16)

    per1 = BLOCK // MAX_DIL
    dil = DILATIONS[1]
    sub = MAX_DIL // dil
    per4 = BLOCK // sub
    bias_ref[0] = mask_t(2 * BLOCK, 3, 4, MAX_DIL, BLOCK)
    bias_ref[1] = mask_t(2 * BLOCK, 3, 4, MAX_DIL, 0)
    bias_ref[2] = mask_t(2 * BLOCK, 5, 6, sub, BLOCK)
    bias_ref[3] = mask_t(2 * BLOCK, 5, 6, sub, 0)
    bias_ref[4, 0:BLOCK, :] = mask_t(BLOCK, 7, 7, 1, 0)
    row = lax.broadcasted_iota(jnp.int32, (gq, BLOCK), 0) & (BLOCK - 1)
    col = lax.broadcasted_iota(jnp.int32, (gq, BLOCK), 1)
    eye_ref[...] = jnp.where(row == col, 1.0, 0.0).astype(BF16)

    def gather(ref, chunks, lanes):
        return jnp.concatenate([ref[0, pl.ds(st, sz), lanes] for st, sz in chunks], axis=0)

    def block(q_chunks, kv_chunks, bias_t, first, last):
        nk = bias_t.shape[0]
        kk = gather(k_ref, kv_chunks, slice(None)).astype(BF16)
        vv = gather(v_ref, kv_chunks, slice(None)).astype(BF16)
        qh = jnp.concatenate(
            [gather(q_ref, q_chunks, slice(g * HEAD_DIM, (g + 1) * HEAD_DIM)) for g in range(GROUP)],
            axis=0).astype(BF16)
        q_aug = jnp.concatenate([qh, eye_ref[...]], axis=1)
        k_aug = jnp.concatenate([kk, bias_t], axis=1)
        v_aug = jnp.concatenate([vv, jnp.ones((nk, LANES), BF16)], axis=1)
        s = lax.dot_general(q_aug, k_aug, (((1,), (1,)), ((), ())), preferred_element_type=F32)
        m_b = jnp.broadcast_to(jnp.max(s, axis=-1, keepdims=True), (gq, LANES))

        def state(ref):
            return jnp.concatenate([gather(ref.at[g:g + 1], q_chunks, slice(None))
                                    for g in range(GROUP)], axis=0)

        if first:
            m_new = m_b
        else:
            m_old = state(m_ref)
            m_new = jnp.maximum(m_old, m_b)
        p = jnp.exp2(s - jnp.concatenate([m_new] * (nk // LANES), axis=1))
        o_aug = jnp.dot(p.astype(BF16), v_aug, preferred_element_type=F32)
        o_b, l_b = o_aug[:, :HEAD_DIM], o_aug[:, HEAD_DIM:]
        if first:
            acc, l = o_b, l_b
        else:
            alpha = jnp.exp2(m_old - m_new)
            acc = alpha * state(acc_ref) + o_b
            l = alpha * state(l_ref) + l_b

        if last:
            out = (acc / l).astype(o_ref.dtype)
            for g in range(GROUP):
                off = g * BLOCK
                for st, sz in q_chunks:
                    o_ref[0, pl.ds(st, sz), g * HEAD_DIM:(g + 1) * HEAD_DIM] = out[off:off + sz]
                    off += sz
        else:
            for g in range(GROUP):
                off = g * BLOCK
                for st, sz in q_chunks:
                    acc_ref[g, pl.ds(st, sz), :] = acc[off:off + sz]
                    m_ref[g, pl.ds(st, sz), :] = m_new[off:off + sz]
                    l_ref[g, pl.ds(st, sz), :] = l[off:off + sz]
                    off += sz

    def body1(nb, carry):
        q0 = pl.multiple_of(nb * per1, per1)
        k0 = pl.multiple_of(jnp.maximum(nb - 1, 0) * per1, per1)
        block([(c * rpc + q0, per1) for c in range(MAX_DIL)],
              [(c * rpc + k0, 2 * per1) for c in range(MAX_DIL)],
              bias_ref[jnp.where(nb > 0, 0, 1)], first=True, last=False)
        return carry

    lax.fori_loop(0, seq // BLOCK, body1, 0, unroll=ATTN_UNROLL)

    nblk4 = seq // (dil * BLOCK)

    def body4(idx, carry):
        r4, n = idx // nblk4, idx % nblk4
        q0 = pl.multiple_of(n * per4, per4)
        k0 = pl.multiple_of(jnp.maximum(n - 1, 0) * per4, per4)
        block([((r4 + dil * c) * rpc + q0, per4) for c in range(sub)],
              [((r4 + dil * c) * rpc + k0, 2 * per4) for c in range(sub)],
              bias_ref[jnp.where(n > 0, 2, 3)], first=False, last=False)
        return carry

    lax.fori_loop(0, dil * nblk4, body4, 0, unroll=ATTN_UNROLL)

    def body16(c, carry):
        r0 = pl.multiple_of(c * rpc, rpc)
        block([(r0, BLOCK)], [(r0, BLOCK)], bias_ref[4, 0:BLOCK, :], first=False, last=True)
        return carry

    lax.fori_loop(0, MAX_DIL, body16, 0, unroll=ATTN_UNROLL)


def _attention(q, kv, *, batch, seq):
    assert DILATIONS == (1, 4, 16) and all(w // d == BLOCK for w, d in zip(WINDOWS, DILATIONS))
    assert seq == MAX_DIL * BLOCK
    qd = N_HEADS * HEAD_DIM
    q3 = q.reshape(batch, seq, qd)
    kv3 = kv.reshape(batch, seq, 2 * KV_DIM)
    gw = GROUP * HEAD_DIM
    o = pl.pallas_call(
        functools.partial(_attn_kernel, seq=seq),
        grid=(batch, N_KV_HEADS),
        in_specs=[
            pl.BlockSpec((1, seq, gw), lambda b, h: (b, 0, h)),
            pl.BlockSpec((1, seq, HEAD_DIM), lambda b, h: (b, 0, h)),
            pl.BlockSpec((1, seq, HEAD_DIM), lambda b, h: (b, 0, N_KV_HEADS + h)),
        ],
        out_specs=pl.BlockSpec((1, seq, gw), lambda b, h: (b, 0, h)),
        out_shape=jax.ShapeDtypeStruct((batch, seq, qd), BF16),
        scratch_shapes=[pltpu.VMEM((GROUP, seq, LANES), F32)] * 3
        + [pltpu.VMEM((GROUP * BLOCK, BLOCK), BF16), pltpu.VMEM((5, 2 * BLOCK, BLOCK), BF16)],
        compiler_params=_params(("parallel", "parallel")),
        name="attn",
    )(q3, kv3, kv3)
    return o.reshape(batch * seq, qd)


def _out_proj_kernel(o_ref, w_ref, h_hbm, out_hbm, h_buf, out_buf, h_sem, out_sem, *, seq):
    tm = o_ref.shape[0]
    i = pl.program_id(0)
    last = pl.num_programs(0) - 1
    slot = _prefetch_residue_tile(h_hbm, h_buf, h_sem, tm=tm, seq=seq)
    write = functools.partial(_residue_tile_copies, out_hbm, out_buf, out_sem, tm=tm, seq=seq, to_hbm=True)

    @pl.when(i >= 2)
    def _():
        for c in write(i - 2, slot):
            c.wait()

    out_buf[slot] = h_buf[slot] + jnp.dot(o_ref[...], w_ref[...], preferred_element_type=F32)
    for c in write(i, slot):
        c.start()

    @pl.when(i == last)
    def _():
        for c in write(i - 1, 1 - slot) + write(i, slot):
            c.wait()


def _out_proj(o, w, h, *, seq, tm=512):
    D = h.shape[-1]
    T, qd = o.shape
    assert T // tm >= 2
    row = lambda i: (i, 0)
    return pl.pallas_call(
        functools.partial(_out_proj_kernel, seq=seq),
        grid=(T // tm,),
        in_specs=[pl.BlockSpec((tm, qd), row), pl.BlockSpec((qd, D), lambda i: (0, 0)),
                  pl.BlockSpec(memory_space=pl.ANY)],
        out_specs=pl.BlockSpec(memory_space=pl.ANY),
        out_shape=jax.ShapeDtypeStruct(h.shape, F32),
        scratch_shapes=[pltpu.VMEM((2, tm, D), F32), pltpu.VMEM((2, tm, D), F32),
                        pltpu.SemaphoreType.DMA((2,)), pltpu.SemaphoreType.DMA((2,))],
        compiler_params=_params(("arbitrary",)),
        name="out_proj",
    )(o, w, h)


def _rope_tables(pos):
    seq = pos.shape[0]
    inv = ROPE_THETA ** (-jnp.arange(0, ROT_DIM, 2, dtype=F32) / ROT_DIM)
    ang = pos[:, None] * inv[None, :]
    cos, sin = jnp.cos(ang), jnp.sin(ang)
    cos_t = jnp.concatenate([cos, cos, jnp.ones((seq, HEAD_DIM - ROT_DIM), F32)], axis=1)
    sin_t = jnp.concatenate([-sin, sin, jnp.zeros((seq, HEAD_DIM - ROT_DIM), F32)], axis=1)
    return cos_t, sin_t


def kernel(x, norm_mix, norm_mlp, conv_w_in, conv_b_in, conv_w_dw, conv_b_dw, conv_ln_g, conv_ln_b,
           conv_w_out, conv_b_out, kv_norm, w_kv, attn_w_q, attn_w_o, mlp_w_in, mlp_w_out, final_norm):
    B, S, D = x.shape
    T = B * S
    assert D == D_MODEL
    row = lambda v: v.reshape(1, -1)
    rows = jnp.arange(S)
    pos = MAX_DIL * (rows % (S // MAX_DIL)) + rows // (S // MAX_DIL)
    cos_t, sin_t = _rope_tables(pos.astype(F32))

    h = x.reshape(T, D)

    h = _conv_mixer(h, row(norm_mix[0]), conv_w_in[0].astype(BF16), row(conv_b_in[0]), conv_w_dw[0],
                    row(conv_b_dw[0]), row(conv_ln_g[0]), row(conv_ln_b[0]),
                    conv_w_out[0].astype(BF16), row(conv_b_out[0]), batch=B, seq=S)
    h = _mlp(h, row(norm_mlp[0]), mlp_w_in, mlp_w_out, row(final_norm), layer=0, final_norm=False)

    h4 = h.reshape(B, S // MAX_DIL, MAX_DIL, D)

    q, kv = _qkv_proj(h4, row(norm_mix[1]), row(kv_norm), attn_w_q[0].astype(BF16), w_kv.astype(BF16),
                      cos_t, sin_t, seq=S, q_scale=LOG2_E / math.sqrt(HEAD_DIM))

    o = _attention(q, kv, batch=B, seq=S)
    h = _out_proj(o, attn_w_o[0].astype(BF16), h4, seq=S).reshape(T, D)
    h = _mlp(h, row(norm_mlp[1]), mlp_w_in, mlp_w_out, row(final_norm), layer=1, final_norm=True)
    return h.reshape(B, S, D)
```

```python
import functools
import math

import jax
import jax.numpy as jnp
from jax import lax
from jax.experimental import pallas as pl
from jax.experimental.pallas import tpu as pltpu

F32 = jnp.float32
BF16 = jnp.bfloat16

D_MODEL = 2048
CONV_WIDTH = 31
HEAD_DIM = 128
N_HEADS = 16
N_KV_HEADS = 4
GROUP = N_HEADS // N_KV_HEADS
KV_DIM = N_KV_HEADS * HEAD_DIM
WINDOWS = (128, 512, 2048)
DILATIONS = (1, 4, 16)
BLOCK = 128
ROPE_THETA = 500000.0
ROT_DIM = HEAD_DIM // 4
D_FF = 4 * D_MODEL
NORM_EPS = 1e-6
LN_EPS = 1e-5

LANES = 128
SUBLANES = 8
HALO = 32
VMEM_LIMIT = 56 * 1024 * 1024
MASK_VALUE = -1e30
LOG2_E = math.log2(math.e)
MAX_DIL = max(DILATIONS)
ATTN_UNROLL = 16


def _params(sem):
    return pltpu.CompilerParams(dimension_semantics=sem, vmem_limit_bytes=VMEM_LIMIT)


def _rmsnorm_bf16(x, g):
    ms = jnp.mean(x * x, axis=-1, keepdims=True)
    return (x * lax.rsqrt(ms + NORM_EPS) * g).astype(BF16)


def _conv_mixer_kernel(x_ref, g_ref, win_ref, bin_ref, wdw_ref, bdw_ref, lng_ref, lnb_ref, wo_ref, bo_ref,
                       o_ref, ext_ref, cv_ref, *, ts, tn):
    i = pl.program_id(1)

    @pl.when(i == 0)
    def _():
        ext_ref[0:HALO, :] = jnp.zeros((HALO, D_MODEL), F32)

    @pl.when(i > 0)
    def _():
        ext_ref[0:HALO, :] = ext_ref[ts:ts + HALO, :]

    x = x_ref[...]
    xn = _rmsnorm_bf16(x, g_ref[...])

    for c in range(D_MODEL // tn):
        cs = slice(c * tn, (c + 1) * tn)
        gs = slice(D_MODEL + c * tn, D_MODEL + (c + 1) * tn)
        a = jnp.dot(xn, win_ref[:, cs], preferred_element_type=F32) + bin_ref[:, cs]
        gate = jnp.dot(xn, win_ref[:, gs], preferred_element_type=F32) + bin_ref[:, gs]
        ext_ref[HALO:HALO + ts, cs] = a * jax.nn.sigmoid(gate)

        for lt in range(c * tn // LANES, (c + 1) * tn // LANES):
            ls = slice(lt * LANES, (lt + 1) * LANES)
            acc = None
            for rho in range(SUBLANES):
                z = None
                for d in range(rho, CONV_WIDTH, SUBLANES):
                    start = HALO - SUBLANES - (d - rho)
                    k = CONV_WIDTH - 1 - d
                    term = ext_ref[start:start + ts + SUBLANES, ls] * wdw_ref[k:k + 1, ls]
                    z = term if z is None else z + term
                part = z[SUBLANES - rho:SUBLANES - rho + ts]
                acc = part if acc is None else acc + part
            cv_ref[:, ls] = acc + bdw_ref[:, ls]

    cv = cv_ref[...]
    mu = jnp.mean(cv, axis=-1, keepdims=True)
    xc = cv - mu
    var = jnp.mean(xc * xc, axis=-1, keepdims=True)
    y = xc * lax.rsqrt(var + LN_EPS) * lng_ref[...] + lnb_ref[...]
    y = (y * jax.nn.sigmoid(y)).astype(BF16)
    o_ref[...] = x + bo_ref[...] + jnp.dot(y, wo_ref[...], preferred_element_type=F32)


def _conv_mixer(h, g, w_in, b_in, w_dw, b_dw, ln_g, ln_b, w_out, b_out, *, batch, seq, ts=512, tn=256):
    D = D_MODEL
    n = seq // ts
    row = lambda b, i: (b * n + i, 0)
    const = lambda b, i: (0, 0)
    resident = lambda shape: pl.BlockSpec(shape, const, pipeline_mode=pl.Buffered(1))
    return pl.pallas_call(
        functools.partial(_conv_mixer_kernel, ts=ts, tn=tn),
        grid=(batch, n),
        in_specs=[
            pl.BlockSpec((ts, D), row),
            resident((1, D)),
            resident((D, 2 * D)),
            resident((1, 2 * D)),
            resident((CONV_WIDTH, D)),
            resident((1, D)),
            resident((1, D)),
            resident((1, D)),
            resident((D, D)),
            resident((1, D)),
        ],
        out_specs=pl.BlockSpec((ts, D), row),
        out_shape=jax.ShapeDtypeStruct((batch * seq, D), F32),
        scratch_shapes=[pltpu.VMEM((HALO + ts, D), F32), pltpu.VMEM((ts, D), F32)],
        compiler_params=_params(("parallel", "arbitrary")),
        name="conv_mixer",
    )(h, g, w_in, b_in, w_dw, b_dw, ln_g, ln_b, w_out, b_out)


def _mlp_kernel(h_ref, g_ref, w1_ref, w2_ref, gf_ref, o_ref, xn_ref, *, final_norm):
    f = pl.program_id(1)

    @pl.when(f == 0)
    def _():
        h = h_ref[...]
        xn_ref[...] = _rmsnorm_bf16(h, g_ref[...])
        o_ref[...] = h

    w1 = w1_ref[...].astype(BF16)
    w2 = w2_ref[...].astype(BF16)
    a = jnp.maximum(jnp.dot(xn_ref[...], w1, preferred_element_type=F32), 0.0)
    o_ref[...] += jnp.dot((a * a).astype(BF16), w2, preferred_element_type=F32)

    if final_norm:
        @pl.when(f == pl.num_programs(1) - 1)
        def _():
            h = o_ref[...]
            ms = jnp.mean(h * h, axis=-1, keepdims=True)
            o_ref[...] = h * lax.rsqrt(ms + NORM_EPS) * gf_ref[...]


def _mlp(h, g, w1, w2, gf, *, layer, final_norm, tm=1024, tf=512):
    T, D = h.shape
    return pl.pallas_call(
        functools.partial(_mlp_kernel, final_norm=final_norm),
        grid=(T // tm, D_FF // tf),
        in_specs=[
            pl.BlockSpec((tm, D), lambda i, f: (i, 0)),
            pl.BlockSpec((1, D), lambda i, f: (0, 0)),
            pl.BlockSpec((None, D, tf), lambda i, f: (layer, 0, f)),
            pl.BlockSpec((None, tf, D), lambda i, f: (layer, f, 0)),
            pl.BlockSpec((1, D), lambda i, f: (0, 0)),
        ],
        out_specs=pl.BlockSpec((tm, D), lambda i, f: (i, 0)),
        out_shape=jax.ShapeDtypeStruct((T, D), F32),
        scratch_shapes=[pltpu.VMEM((tm, D), BF16)],
        compiler_params=_params(("parallel", "arbitrary")),
        name="mlp",
    )(h, g, w1, w2, gf)


def _rope_tile(x, cos_t, sin_t):
    lane = lax.broadcasted_iota(jnp.int32, (x.shape[0], HEAD_DIM), 1)
    first_half = lane < ROT_DIM // 2
    outs = []
    for hh in range(x.shape[1] // HEAD_DIM):
        xh = x[:, hh * HEAD_DIM:(hh + 1) * HEAD_DIM]
        partner = jnp.where(first_half,
                            pltpu.roll(xh, HEAD_DIM - ROT_DIM // 2, 1),
                            pltpu.roll(xh, ROT_DIM // 2, 1))
        outs.append(xh * cos_t + partner * sin_t)
    return jnp.concatenate(outs, axis=1)


def _residue_tile_copies(hbm_ref, buf_ref, sem_ref, tile, slot, *, tm, seq, to_hbm):
    rpc = seq // MAX_DIL
    tiles_per_seq = seq // tm
    b = tile // tiles_per_seq
    c0 = (tile % tiles_per_seq) * (tm // rpc)
    copies = []
    for k in range(tm // rpc):
        hbm_rows = hbm_ref.at[b, :, c0 + k, :]
        vmem_rows = buf_ref.at[slot, pl.ds(k * rpc, rpc), :]
        src, dst = (vmem_rows, hbm_rows) if to_hbm else (hbm_rows, vmem_rows)
        copies.append(pltpu.make_async_copy(src, dst, sem_ref.at[slot]))
    return copies


def _prefetch_residue_tile(hbm_ref, buf_ref, sem_ref, *, tm, seq):
    i = pl.program_id(0)
    slot = i % 2
    fetch = functools.partial(_residue_tile_copies, hbm_ref, buf_ref, sem_ref, tm=tm, seq=seq, to_hbm=False)

    @pl.when(i == 0)
    def _():
        for c in fetch(0, 0):
            c.start()

    @pl.when(i + 1 < pl.num_programs(0))
    def _():
        for c in fetch(i + 1, 1 - slot):
            c.start()

    for c in fetch(i, slot):
        c.wait()
    return slot


def _qkv_kernel(x_hbm, gq_ref, gkv_ref, wq_ref, wkv_ref, cos_ref, sin_ref, q_ref, kv_ref, x_buf, x_sem,
                *, q_scale, sub, tn, seq):
    tm = q_ref.shape[0]
    slot = _prefetch_residue_tile(x_hbm, x_buf, x_sem, tm=tm, seq=seq)
    for r in range(tm // sub):
        rs = slice(r * sub, (r + 1) * sub)
        x = x_buf[slot, rs, :]
        xs = x * lax.rsqrt(jnp.mean(x * x, axis=-1, keepdims=True) + NORM_EPS)
        xq = (xs * gq_ref[...]).astype(BF16)
        xkv = (xs * gkv_ref[...]).astype(BF16)
        cos_t, sin_t = cos_ref[rs, :], sin_ref[rs, :]
        for c in range(q_ref.shape[1] // tn):
            cs = slice(c * tn, (c + 1) * tn)
            acc = jnp.dot(xq, wq_ref[:, cs], preferred_element_type=F32)
            q_ref[rs, cs] = _rope_tile(acc, cos_t, sin_t) * q_scale
        k = jnp.dot(xkv, wkv_ref[:, :KV_DIM], preferred_element_type=F32)
        kv_ref[rs, :KV_DIM] = _rope_tile(k, cos_t, sin_t)
        kv_ref[rs, KV_DIM:] = jnp.dot(xkv, wkv_ref[:, KV_DIM:], preferred_element_type=F32)


def _qkv_proj(x, g_q, g_kv, w_q, w_kv, cos_t, sin_t, *, seq, q_scale, tm=512, sub=256, tn=512):
    D = x.shape[-1]
    T = x.shape[0] * seq
    qd, kvd = w_q.shape[1], w_kv.shape[1]
    spb = seq // tm
    row = lambda i: (i, 0)
    const = lambda i: (0, 0)
    table = pl.BlockSpec((tm, HEAD_DIM), lambda i: (i % spb, 0))
    return pl.pallas_call(
        functools.partial(_qkv_kernel, q_scale=q_scale, sub=sub, tn=tn, seq=seq),
        grid=(T // tm,),
        in_specs=[
            pl.BlockSpec(memory_space=pl.ANY),
            pl.BlockSpec((1, D), const),
            pl.BlockSpec((1, D), const),
            pl.BlockSpec((D, qd), const),
            pl.BlockSpec((D, kvd), const),
            table, table,
        ],
        out_specs=[pl.BlockSpec((tm, qd), row), pl.BlockSpec((tm, kvd), row)],
        out_shape=[jax.ShapeDtypeStruct((T, qd), F32), jax.ShapeDtypeStruct((T, kvd), F32)],
        scratch_shapes=[pltpu.VMEM((2, tm, D), F32), pltpu.SemaphoreType.DMA((2,))],
        compiler_params=_params(("arbitrary",)),
        name="qkv_proj",
    )(x, g_q, g_kv, w_q, w_kv, cos_t, sin_t)


def _attn_kernel(q_ref, k_ref, v_ref, o_ref, acc_ref, m_ref, l_ref, eye_ref, bias_ref, *, seq):
    rpc = seq // MAX_DIL
    gq = GROUP * BLOCK

    def mask_t(nk, q_major_shift, k_major_shift, minor_scale, base):
        j = lax.broadcasted_iota(jnp.int32, (nk, BLOCK), 0)
        e = lax.broadcasted_iota(jnp.int32, (nk, BLOCK), 1)
        qmaj, qmin = e >> q_major_shift, e & ((1 << q_major_shift) - 1)
        kmaj, kmin = j >> k_major_shift, j & ((1 << k_major_shift) - 1)
        dist = base + minor_scale * (qmin - kmin) + (qmaj - kmaj)
        return jnp.where(jnp.logical_and(dist >= 0, dist <= BLOCK), 0.0, MASK_VALUE).astype(BF16)

    per1 = BLOCK // MAX_DIL
    dil = DILATIONS[1]
    sub = MAX_DIL // dil
    per4 = BLOCK // sub
    bias_ref[0] = mask_t(2 * BLOCK, 3, 4, MAX_DIL, BLOCK)
    bias_ref[1] = mask_t(2 * BLOCK, 3, 4, MAX_DIL, 0)
    bias_ref[2] = mask_t(2 * BLOCK, 5, 6, sub, BLOCK)
    bias_ref[3] = mask_t(2 * BLOCK, 5, 6, sub, 0)
    bias_ref[4, 0:BLOCK, :] = mask_t(BLOCK, 7, 7, 1, 0)
    row = lax.broadcasted_iota(jnp.int32, (gq, BLOCK), 0) & (BLOCK - 1)
    col = lax.broadcasted_iota(jnp.int32, (gq, BLOCK), 1)
    eye_ref[...] = jnp.where(row == col, 1.0, 0.0).astype(BF16)

    def gather(ref, chunks, lanes):
        return jnp.concatenate([ref[0, pl.ds(st, sz), lanes] for st, sz in chunks], axis=0)

    def block(q_chunks, kv_chunks, bias_t, first, last):
        nk = bias_t.shape[0]
        kk = gather(k_ref, kv_chunks, slice(None)).astype(BF16)
        vv = gather(v_ref, kv_chunks, slice(None)).astype(BF16)
        qh = jnp.concatenate(
            [gather(q_ref, q_chunks, slice(g * HEAD_DIM, (g + 1) * HEAD_DIM)) for g in range(GROUP)],
            axis=0).astype(BF16)
        q_aug = jnp.concatenate([qh, eye_ref[...]], axis=1)
        k_aug = jnp.concatenate([kk, bias_t], axis=1)
        v_aug = jnp.concatenate([vv, jnp.ones((nk, LANES), BF16)], axis=1)
        s = lax.dot_general(q_aug, k_aug, (((1,), (1,)), ((), ())), preferred_element_type=F32)
        m_b = jnp.broadcast_to(jnp.max(s, axis=-1, keepdims=True), (gq, LANES))

        def state(ref):
            return jnp.concatenate([gather(ref.at[g:g + 1], q_chunks, slice(None))
                                    for g in range(GROUP)], axis=0)

        if first:
            m_new = m_b
        else:
            m_old = state(m_ref)
            m_new = jnp.maximum(m_old, m_b)
        p = jnp.exp2(s - jnp.concatenate([m_new] * (nk // LANES), axis=1))
        o_aug = jnp.dot(p.astype(BF16), v_aug, preferred_element_type=F32)
        o_b, l_b = o_aug[:, :HEAD_DIM], o_aug[:, HEAD_DIM:]
        if first:
            acc, l = o_b, l_b
        else:
            alpha = jnp.exp2(m_old - m_new)
            acc = alpha * state(acc_ref) + o_b
            l = alpha * state(l_ref) + l_b

        if last:
            out = (acc / l).astype(o_ref.dtype)
            for g in range(GROUP):
                off = g * BLOCK
                for st, sz in q_chunks:
                    o_ref[0, pl.ds(st, sz), g * HEAD_DIM:(g + 1) * HEAD_DIM] = out[off:off + sz]
                    off += sz
        else:
            for g in range(GROUP):
                off = g * BLOCK
                for st, sz in q_chunks:
                    acc_ref[g, pl.ds(st, sz), :] = acc[off:off + sz]
                    m_ref[g, pl.ds(st, sz), :] = m_new[off:off + sz]
                    l_ref[g, pl.ds(st, sz), :] = l[off:off + sz]
                    off += sz

    def body1(nb, carry):
        q0 = pl.multiple_of(nb * per1, per1)
        k0 = pl.multiple_of(jnp.maximum(nb - 1, 0) * per1, per1)
        block([(c * rpc + q0, per1) for c in range(MAX_DIL)],
              [(c * rpc + k0, 2 * per1) for c in range(MAX_DIL)],
              bias_ref[jnp.where(nb > 0, 0, 1)], first=True, last=False)
        return carry

    lax.fori_loop(0, seq // BLOCK, body1, 0, unroll=ATTN_UNROLL)

    nblk4 = seq // (dil * BLOCK)

    def body4(idx, carry):
        r4, n = idx // nblk4, idx % nblk4
        q0 = pl.multiple_of(n * per4, per4)
        k0 = pl.multiple_of(jnp.maximum(n - 1, 0) * per4, per4)
        block([((r4 + dil * c) * rpc + q0, per4) for c in range(sub)],
              [((r4 + dil * c) * rpc + k0, 2 * per4) for c in range(sub)],
              bias_ref[jnp.where(n > 0, 2, 3)], first=False, last=False)
        return carry

    lax.fori_loop(0, dil * nblk4, body4, 0, unroll=ATTN_UNROLL)

    def body16(c, carry):
        r0 = pl.multiple_of(c * rpc, rpc)
        block([(r0, BLOCK)], [(r0, BLOCK)], bias_ref[4, 0:BLOCK, :], first=False, last=True)
        return carry

    lax.fori_loop(0, MAX_DIL, body16, 0, unroll=ATTN_UNROLL)


def _attention(q, kv, *, batch, seq):
    assert DILATIONS == (1, 4, 16) and all(w // d == BLOCK for w, d in zip(WINDOWS, DILATIONS))
    assert seq == MAX_DIL * BLOCK
    qd = N_HEADS * HEAD_DIM
    q3 = q.reshape(batch, seq, qd)
    kv3 = kv.reshape(batch, seq, 2 * KV_DIM)
    gw = GROUP * HEAD_DIM
    o = pl.pallas_call(
        functools.partial(_attn_kernel, seq=seq),
        grid=(batch, N_KV_HEADS),
        in_specs=[
            pl.BlockSpec((1, seq, gw), lambda b, h: (b, 0, h)),
            pl.BlockSpec((1, seq, HEAD_DIM), lambda b, h: (b, 0, h)),
            pl.BlockSpec((1, seq, HEAD_DIM), lambda b, h: (b, 0, N_KV_HEADS + h)),
        ],
        out_specs=pl.BlockSpec((1, seq, gw), lambda b, h: (b, 0, h)),
        out_shape=jax.ShapeDtypeStruct((batch, seq, qd), BF16),
        scratch_shapes=[pltpu.VMEM((GROUP, seq, LANES), F32)] * 3
        + [pltpu.VMEM((GROUP * BLOCK, BLOCK), BF16), pltpu.VMEM((5, 2 * BLOCK, BLOCK), BF16)],
        compiler_params=_params(("parallel", "parallel")),
        name="attn",
    )(q3, kv3, kv3)
    return o.reshape(batch * seq, qd)


def _out_proj_kernel(o_ref, w_ref, h_hbm, out_hbm, h_buf, out_buf, h_sem, out_sem, *, seq):
    tm = o_ref.shape[0]
    i = pl.program_id(0)
    last = pl.num_programs(0) - 1
    slot = _prefetch_residue_tile(h_hbm, h_buf, h_sem, tm=tm, seq=seq)
    write = functools.partial(_residue_tile_copies, out_hbm, out_buf, out_sem, tm=tm, seq=seq, to_hbm=True)

    @pl.when(i >= 2)
    def _():
        for c in write(i - 2, slot):
            c.wait()

    out_buf[slot] = h_buf[slot] + jnp.dot(o_ref[...], w_ref[...], preferred_element_type=F32)
    for c in write(i, slot):
        c.start()

    @pl.when(i == last)
    def _():
        for c in write(i - 1, 1 - slot) + write(i, slot):
            c.wait()


def _out_proj(o, w, h, *, seq, tm=512):
    D = h.shape[-1]
    T, qd = o.shape
    assert T // tm >= 2
    row = lambda i: (i, 0)
    return pl.pallas_call(
        functools.partial(_out_proj_kernel, seq=seq),
        grid=(T // tm,),
        in_specs=[pl.BlockSpec((tm, qd), row), pl.BlockSpec((qd, D), lambda i: (0, 0)),
                  pl.BlockSpec(memory_space=pl.ANY)],
        out_specs=pl.BlockSpec(memory_space=pl.ANY),
        out_shape=jax.ShapeDtypeStruct(h.shape, F32),
        scratch_shapes=[pltpu.VMEM((2, tm, D), F32), pltpu.VMEM((2, tm, D), F32),
                        pltpu.SemaphoreType.DMA((2,)), pltpu.SemaphoreType.DMA((2,))],
        compiler_params=_params(("arbitrary",)),
        name="out_proj",
    )(o, w, h)


def _rope_tables(pos):
    seq = pos.shape[0]
    inv = ROPE_THETA ** (-jnp.arange(0, ROT_DIM, 2, dtype=F32) / ROT_DIM)
    ang = pos[:, None] * inv[None, :]
    cos, sin = jnp.cos(ang), jnp.sin(ang)
    cos_t = jnp.concatenate([cos, cos, jnp.ones((seq, HEAD_DIM - ROT_DIM), F32)], axis=1)
    sin_t = jnp.concatenate([-sin, sin, jnp.zeros((seq, HEAD_DIM - ROT_DIM), F32)], axis=1)
    return cos_t, sin_t


def kernel(x, norm_mix, norm_mlp, conv_w_in, conv_b_in, conv_w_dw, conv_b_dw, conv_ln_g, conv_ln_b,
           conv_w_out, conv_b_out, kv_norm, w_kv, attn_w_q, attn_w_o, mlp_w_in, mlp_w_out, final_norm):
    B, S, D = x.shape
    T = B * S
    assert D == D_MODEL
    row = lambda v: v.reshape(1, -1)
    rows = jnp.arange(S)
    pos = MAX_DIL * (rows % (S // MAX_DIL)) + rows // (S // MAX_DIL)
    cos_t, sin_t = _rope_tables(pos.astype(F32))

    h = x.reshape(T, D)

    h = _conv_mixer(h, row(norm_mix[0]), conv_w_in[0].astype(BF16), row(conv_b_in[0]), conv_w_dw[0],
                    row(conv_b_dw[0]), row(conv_ln_g[0]), row(conv_ln_b[0]),
                    conv_w_out[0].astype(BF16), row(conv_b_out[0]), batch=B, seq=S)
    h = _mlp(h, row(norm_mlp[0]), mlp_w_in, mlp_w_out, row(final_norm), layer=0, final_norm=False)

    h4 = h.reshape(B, S // MAX_DIL, MAX_DIL, D)

    q, kv = _qkv_proj(h4, row(norm_mix[1]), row(kv_norm), attn_w_q[0].astype(BF16), w_kv.astype(BF16),
                      cos_t, sin_t, seq=S, q_scale=LOG2_E / math.sqrt(HEAD_DIM))

    o = _attention(q, kv, batch=B, seq=S)
    h = _out_proj(o, attn_w_o[0].astype(BF16), h4, seq=S).reshape(T, D)
    h = _mlp(h, row(norm_mlp[1]), mlp_w_in, mlp_w_out, row(final_norm), layer=1, final_norm=True)
    return h.reshape(B, S, D)
```

```python
import functools
import math

import jax
import jax.numpy as jnp
from jax import lax
from jax.experimental import pallas as pl
from jax.experimental.pallas import tpu as pltpu

F32 = jnp.float32
BF16 = jnp.bfloat16

D_MODEL = 2048
CONV_WIDTH = 31
HEAD_DIM = 128
N_HEADS = 16
N_KV_HEADS = 4
GROUP = N_HEADS // N_KV_HEADS
KV_DIM = N_KV_HEADS * HEAD_DIM
WINDOWS = (128, 512, 2048)
DILATIONS = (1, 4, 16)
BLOCK = 128
ROPE_THETA = 500000.0
ROT_DIM = HEAD_DIM // 4
D_FF = 4 * D_MODEL
NORM_EPS = 1e-6
LN_EPS = 1e-5

LANES = 128
SUBLANES = 8
HALO = 32
VMEM_LIMIT = 56 * 1024 * 1024
MASK_VALUE = -1e30
LOG2_E = math.log2(math.e)
MAX_DIL = max(DILATIONS)
ATTN_UNROLL = 16


def _params(sem):
    return pltpu.CompilerParams(dimension_semantics=sem, vmem_limit_bytes=VMEM_LIMIT)


def _rmsnorm_bf16(x, g):
    ms = jnp.mean(x * x, axis=-1, keepdims=True)
    return (x * lax.rsqrt(ms + NORM_EPS) * g).astype(BF16)


def _conv_mixer_kernel(x_ref, g_ref, win_ref, bin_ref, wdw_ref, bdw_ref, lng_ref, lnb_ref, wo_ref, bo_ref,
                       o_ref, ext_ref, cv_ref, *, ts, tn):
    i = pl.program_id(1)

    @pl.when(i == 0)
    def _():
        ext_ref[0:HALO, :] = jnp.zeros((HALO, D_MODEL), F32)

    @pl.when(i > 0)
    def _():
        ext_ref[0:HALO, :] = ext_ref[ts:ts + HALO, :]

    x = x_ref[...]
    xn = _rmsnorm_bf16(x, g_ref[...])

    for c in range(D_MODEL // tn):
        cs = slice(c * tn, (c + 1) * tn)
        gs = slice(D_MODEL + c * tn, D_MODEL + (c + 1) * tn)
        a = jnp.dot(xn, win_ref[:, cs], preferred_element_type=F32) + bin_ref[:, cs]
        gate = jnp.dot(xn, win_ref[:, gs], preferred_element_type=F32) + bin_ref[:, gs]
        ext_ref[HALO:HALO + ts, cs] = a * jax.nn.sigmoid(gate)

        for lt in range(c * tn // LANES, (c + 1) * tn // LANES):
            ls = slice(lt * LANES, (lt + 1) * LANES)
            acc = None
            for rho in range(SUBLANES):
                z = None
                for d in range(rho, CONV_WIDTH, SUBLANES):
                    start = HALO - SUBLANES - (d - rho)
                    k = CONV_WIDTH - 1 - d
                    term = ext_ref[start:start + ts + SUBLANES, ls] * wdw_ref[k:k + 1, ls]
                    z = term if z is None else z + term
                part = z[SUBLANES - rho:SUBLANES - rho + ts]
                acc = part if acc is None else acc + part
            cv_ref[:, ls] = acc + bdw_ref[:, ls]

    cv = cv_ref[...]
    mu = jnp.mean(cv, axis=-1, keepdims=True)
    xc = cv - mu
    var = jnp.mean(xc * xc, axis=-1, keepdims=True)
    y = xc * lax.rsqrt(var + LN_EPS) * lng_ref[...] + lnb_ref[...]
    y = (y * jax.nn.sigmoid(y)).astype(BF16)
    o_ref[...] = x + bo_ref[...] + jnp.dot(y, wo_ref[...], preferred_element_type=F32)


def _conv_mixer(h, g, w_in, b_in, w_dw, b_dw, ln_g, ln_b, w_out, b_out, *, batch, seq, ts=512, tn=256):
    D = D_MODEL
    n = seq // ts
    row = lambda b, i: (b * n + i, 0)
    const = lambda b, i: (0, 0)
    resident = lambda shape: pl.BlockSpec(shape, const, pipeline_mode=pl.Buffered(1))
    return pl.pallas_call(
        functools.partial(_conv_mixer_kernel, ts=ts, tn=tn),
        grid=(batch, n),
        in_specs=[
            pl.BlockSpec((ts, D), row),
            resident((1, D)),
            resident((D, 2 * D)),
            resident((1, 2 * D)),
            resident((CONV_WIDTH, D)),
            resident((1, D)),
            resident((1, D)),
            resident((1, D)),
            resident((D, D)),
            resident((1, D)),
        ],
        out_specs=pl.BlockSpec((ts, D), row),
        out_shape=jax.ShapeDtypeStruct((batch * seq, D), F32),
        scratch_shapes=[pltpu.VMEM((HALO + ts, D), F32), pltpu.VMEM((ts, D), F32)],
        compiler_params=_params(("parallel", "arbitrary")),
        name="conv_mixer",
    )(h, g, w_in, b_in, w_dw, b_dw, ln_g, ln_b, w_out, b_out)


def _mlp_kernel(*refs, final_norm, parts):
    h_refs = refs[:parts]
    g_ref, w1_ref, w2_ref, gf_ref, o_ref, xn_ref = refs[parts:]
    f = pl.program_id(1)

    @pl.when(f == 0)
    def _():
        hs = [r[...] for r in h_refs]
        ms = sum(jnp.sum(h * h, axis=-1, keepdims=True) for h in hs) * (1.0 / D_MODEL)
        inv = lax.rsqrt(ms + NORM_EPS)
        width = D_MODEL // parts
        for k, h in enumerate(hs):
            cs = slice(k * width, (k + 1) * width)
            xn_ref[:, cs] = (h * inv * g_ref[:, cs]).astype(BF16)
            o_ref[:, cs] = h

    w1 = w1_ref[...].astype(BF16)
    w2 = w2_ref[...].astype(BF16)
    a = jnp.maximum(jnp.dot(xn_ref[...], w1, preferred_element_type=F32), 0.0)
    o_ref[...] += jnp.dot((a * a).astype(BF16), w2, preferred_element_type=F32)

    if final_norm:
        @pl.when(f == pl.num_programs(1) - 1)
        def _():
            h = o_ref[...]
            ms = jnp.mean(h * h, axis=-1, keepdims=True)
            o_ref[...] = h * lax.rsqrt(ms + NORM_EPS) * gf_ref[...]


def _mlp(h, g, w1, w2, gf, *, layer, final_norm, tm=1024, tf=512, parts=4):
    T, D = h.shape
    ni, nf = T // tm, D_FF // tf

    def slab(k):
        def index(i, f):
            return jnp.where(f >= nf - k, jnp.minimum(i + 1, ni - 1), i), k
        return pl.BlockSpec((tm, D // parts), index)

    return pl.pallas_call(
        functools.partial(_mlp_kernel, final_norm=final_norm, parts=parts),
        grid=(ni, nf),
        in_specs=[slab(k) for k in range(parts)] + [
            pl.BlockSpec((1, D), lambda i, f: (0, 0)),
            pl.BlockSpec((None, D, tf), lambda i, f: (layer, 0, f)),
            pl.BlockSpec((None, tf, D), lambda i, f: (layer, f, 0)),
            pl.BlockSpec((1, D), lambda i, f: (0, 0)),
        ],
        out_specs=pl.BlockSpec((tm, D), lambda i, f: (i, 0)),
        out_shape=jax.ShapeDtypeStruct((T, D), F32),
        scratch_shapes=[pltpu.VMEM((tm, D), BF16)],
        compiler_params=_params(("parallel", "arbitrary")),
        name="mlp",
    )(*([h] * parts), g, w1, w2, gf)


def _rope_tile(x, cos_t, sin_t):
    lane = lax.broadcasted_iota(jnp.int32, (x.shape[0], HEAD_DIM), 1)
    first_half = lane < ROT_DIM // 2
    outs = []
    for hh in range(x.shape[1] // HEAD_DIM):
        xh = x[:, hh * HEAD_DIM:(hh + 1) * HEAD_DIM]
        partner = jnp.where(first_half,
                            pltpu.roll(xh, HEAD_DIM - ROT_DIM // 2, 1),
                            pltpu.roll(xh, ROT_DIM // 2, 1))
        outs.append(xh * cos_t + partner * sin_t)
    return jnp.concatenate(outs, axis=1)


def _residue_tile_copies(hbm_ref, buf_ref, sem_ref, tile, slot, *, tm, seq, to_hbm):
    rpc = seq // MAX_DIL
    tiles_per_seq = seq // tm
    b = tile // tiles_per_seq
    c0 = (tile % tiles_per_seq) * (tm // rpc)
    copies = []
    for k in range(tm // rpc):
        hbm_rows = hbm_ref.at[b, :, c0 + k, :]
        vmem_rows = buf_ref.at[slot, pl.ds(k * rpc, rpc), :]
        src, dst = (vmem_rows, hbm_rows) if to_hbm else (hbm_rows, vmem_rows)
        copies.append(pltpu.make_async_copy(src, dst, sem_ref.at[slot]))
    return copies


def _prefetch_residue_tile(hbm_ref, buf_ref, sem_ref, *, tm, seq):
    i = pl.program_id(0)
    slot = i % 2
    fetch = functools.partial(_residue_tile_copies, hbm_ref, buf_ref, sem_ref, tm=tm, seq=seq, to_hbm=False)

    @pl.when(i == 0)
    def _():
        for c in fetch(0, 0):
            c.start()

    @pl.when(i + 1 < pl.num_programs(0))
    def _():
        for c in fetch(i + 1, 1 - slot):
            c.start()

    for c in fetch(i, slot):
        c.wait()
    return slot


def _qkv_kernel(x_hbm, gq_ref, gkv_ref, wq_ref, wkv_ref, cos_ref, sin_ref, q_ref, kv_ref, x_buf, x_sem,
                *, q_scale, sub, tn, seq):
    tm = q_ref.shape[0]
    slot = _prefetch_residue_tile(x_hbm, x_buf, x_sem, tm=tm, seq=seq)
    for r in range(tm // sub):
        rs = slice(r * sub, (r + 1) * sub)
        x = x_buf[slot, rs, :]
        xs = x * lax.rsqrt(jnp.mean(x * x, axis=-1, keepdims=True) + NORM_EPS)
        xq = (xs * gq_ref[...]).astype(BF16)
        xkv = (xs * gkv_ref[...]).astype(BF16)
        cos_t, sin_t = cos_ref[rs, :], sin_ref[rs, :]
        for c in range(q_ref.shape[1] // tn):
            cs = slice(c * tn, (c + 1) * tn)
            acc = jnp.dot(xq, wq_ref[:, cs], preferred_element_type=F32)
            q_ref[rs, cs] = _rope_tile(acc, cos_t, sin_t) * q_scale
        k = jnp.dot(xkv, wkv_ref[:, :KV_DIM], preferred_element_type=F32)
        kv_ref[rs, :KV_DIM] = _rope_tile(k, cos_t, sin_t)
        kv_ref[rs, KV_DIM:] = jnp.dot(xkv, wkv_ref[:, KV_DIM:], preferred_element_type=F32)


def _qkv_proj(x, g_q, g_kv, w_q, w_kv, cos_t, sin_t, *, seq, q_scale, tm=512, sub=256, tn=512):
    D = x.shape[-1]
    T = x.shape[0] * seq
    qd, kvd = w_q.shape[1], w_kv.shape[1]
    spb = seq // tm
    row = lambda i: (i, 0)
    const = lambda i: (0, 0)
    table = pl.BlockSpec((tm, HEAD_DIM), lambda i: (i % spb, 0))
    return pl.pallas_call(
        functools.partial(_qkv_kernel, q_scale=q_scale, sub=sub, tn=tn, seq=seq),
        grid=(T // tm,),
        in_specs=[
            pl.BlockSpec(memory_space=pl.ANY),
            pl.BlockSpec((1, D), const),
            pl.BlockSpec((1, D), const),
            pl.BlockSpec((D, qd), const),
            pl.BlockSpec((D, kvd), const),
            table, table,
        ],
        out_specs=[pl.BlockSpec((tm, qd), row), pl.BlockSpec((tm, kvd), row)],
        out_shape=[jax.ShapeDtypeStruct((T, qd), F32), jax.ShapeDtypeStruct((T, kvd), F32)],
        scratch_shapes=[pltpu.VMEM((2, tm, D), F32), pltpu.SemaphoreType.DMA((2,))],
        compiler_params=_params(("arbitrary",)),
        name="qkv_proj",
    )(x, g_q, g_kv, w_q, w_kv, cos_t, sin_t)


def _attn_kernel(q_ref, k_ref, v_ref, o_ref, acc_ref, m_ref, l_ref, eye_ref, bias_ref, *, seq):
    rpc = seq // MAX_DIL
    gq = GROUP * BLOCK

    def mask_t(nk, q_major_shift, k_major_shift, minor_scale, base):
        j = lax.broadcasted_iota(jnp.int32, (nk, BLOCK), 0)
        e = lax.broadcasted_iota(jnp.int32, (nk, BLOCK), 1)
        qmaj, qmin = e >> q_major_shift, e & ((1 << q_major_shift) - 1)
        kmaj, kmin = j >> k_major_shift, j & ((1 << k_major_shift) - 1)
        dist = base + minor_scale * (qmin - kmin) + (qmaj - kmaj)
        return jnp.where(jnp.logical_and(dist >= 0, dist <= BLOCK), 0.0, MASK_VALUE).astype(BF16)

    per1 = BLOCK // MAX_DIL
    dil = DILATIONS[1]
    sub = MAX_DIL // dil
    per4 = BLOCK // sub
    bias_ref[0] = mask_t(2 * BLOCK, 3, 4, MAX_DIL, BLOCK)
    bias_ref[1] = mask_t(2 * BLOCK, 3, 4, MAX_DIL, 0)
    bias_ref[2] = mask_t(2 * BLOCK, 5, 6, sub, BLOCK)
    bias_ref[3] = mask_t(2 * BLOCK, 5, 6, sub, 0)
    bias_ref[4, 0:BLOCK, :] = mask_t(BLOCK, 7, 7, 1, 0)
    row = lax.broadcasted_iota(jnp.int32, (gq, BLOCK), 0) & (BLOCK - 1)
    col = lax.broadcasted_iota(jnp.int32, (gq, BLOCK), 1)
    eye_ref[...] = jnp.where(row == col, 1.0, 0.0).astype(BF16)

    def gather(ref, chunks, lanes):
        return jnp.concatenate([ref[0, pl.ds(st, sz), lanes] for st, sz in chunks], axis=0)

    def block(q_chunks, kv_chunks, bias_t, first, last):
        nk = bias_t.shape[0]
        kk = gather(k_ref, kv_chunks, slice(None)).astype(BF16)
        vv = gather(v_ref, kv_chunks, slice(None)).astype(BF16)
        qh = jnp.concatenate(
            [gather(q_ref, q_chunks, slice(g * HEAD_DIM, (g + 1) * HEAD_DIM)) for g in range(GROUP)],
            axis=0).astype(BF16)
        q_aug = jnp.concatenate([qh, eye_ref[...]], axis=1)
        k_aug = jnp.concatenate([kk, bias_t], axis=1)
        v_aug = jnp.concatenate([vv, jnp.ones((nk, LANES), BF16)], axis=1)
        s = lax.dot_general(q_aug, k_aug, (((1,), (1,)), ((), ())), preferred_element_type=F32)
        m_b = jnp.broadcast_to(jnp.max(s, axis=-1, keepdims=True), (gq, LANES))

        def state(ref):
            return jnp.concatenate([gather(ref.at[g:g + 1], q_chunks, slice(None))
                                    for g in range(GROUP)], axis=0)

        if first:
            m_new = m_b
        else:
            m_old = state(m_ref)
            m_new = jnp.maximum(m_old, m_b)
        p = jnp.exp2(s - jnp.concatenate([m_new] * (nk // LANES), axis=1))
        o_aug = jnp.dot(p.astype(BF16), v_aug, preferred_element_type=F32)
        o_b, l_b = o_aug[:, :HEAD_DIM], o_aug[:, HEAD_DIM:]
        if first:
            acc, l = o_b, l_b
        else:
            alpha = jnp.exp2(m_old - m_new)
            acc = alpha * state(acc_ref) + o_b
            l = alpha * state(l_ref) + l_b

        if last:
            out = (acc / l).astype(o_ref.dtype)
            for g in range(GROUP):
                off = g * BLOCK
                for st, sz in q_chunks:
                    o_ref[0, pl.ds(st, sz), g * HEAD_DIM:(g + 1) * HEAD_DIM] = out[off:off + sz]
                    off += sz
        else:
            for g in range(GROUP):
                off = g * BLOCK
                for st, sz in q_chunks:
                    acc_ref[g, pl.ds(st, sz), :] = acc[off:off + sz]
                    m_ref[g, pl.ds(st, sz), :] = m_new[off:off + sz]
                    l_ref[g, pl.ds(st, sz), :] = l[off:off + sz]
                    off += sz

    def body1(nb, carry):
        q0 = pl.multiple_of(nb * per1, per1)
        k0 = pl.multiple_of(jnp.maximum(nb - 1, 0) * per1, per1)
        block([(c * rpc + q0, per1) for c in range(MAX_DIL)],
              [(c * rpc + k0, 2 * per1) for c in range(MAX_DIL)],
              bias_ref[jnp.where(nb > 0, 0, 1)], first=True, last=False)
        return carry

    lax.fori_loop(0, seq // BLOCK, body1, 0, unroll=ATTN_UNROLL)

    nblk4 = seq // (dil * BLOCK)

    def body4(idx, carry):
        r4, n = idx // nblk4, idx % nblk4
        q0 = pl.multiple_of(n * per4, per4)
        k0 = pl.multiple_of(jnp.maximum(n - 1, 0) * per4, per4)
        block([((r4 + dil * c) * rpc + q0, per4) for c in range(sub)],
              [((r4 + dil * c) * rpc + k0, 2 * per4) for c in range(sub)],
              bias_ref[jnp.where(n > 0, 2, 3)], first=False, last=False)
        return carry

    lax.fori_loop(0, dil * nblk4, body4, 0, unroll=ATTN_UNROLL)

    def body16(c, carry):
        r0 = pl.multiple_of(c * rpc, rpc)
        block([(r0, BLOCK)], [(r0, BLOCK)], bias_ref[4, 0:BLOCK, :], first=False, last=True)
        return carry

    lax.fori_loop(0, MAX_DIL, body16, 0, unroll=ATTN_UNROLL)


def _attention(q, kv, *, batch, seq):
    assert DILATIONS == (1, 4, 16) and all(w // d == BLOCK for w, d in zip(WINDOWS, DILATIONS))
    assert seq == MAX_DIL * BLOCK
    qd = N_HEADS * HEAD_DIM
    q3 = q.reshape(batch, seq, qd)
    kv3 = kv.reshape(batch, seq, 2 * KV_DIM)
    gw = GROUP * HEAD_DIM
    o = pl.pallas_call(
        functools.partial(_attn_kernel, seq=seq),
        grid=(batch, N_KV_HEADS),
        in_specs=[
            pl.BlockSpec((1, seq, gw), lambda b, h: (b, 0, h)),
            pl.BlockSpec((1, seq, HEAD_DIM), lambda b, h: (b, 0, h)),
            pl.BlockSpec((1, seq, HEAD_DIM), lambda b, h: (b, 0, N_KV_HEADS + h)),
        ],
        out_specs=pl.BlockSpec((1, seq, gw), lambda b, h: (b, 0, h)),
        out_shape=jax.ShapeDtypeStruct((batch, seq, qd), BF16),
        scratch_shapes=[pltpu.VMEM((GROUP, seq, LANES), F32)] * 3
        + [pltpu.VMEM((GROUP * BLOCK, BLOCK), BF16), pltpu.VMEM((5, 2 * BLOCK, BLOCK), BF16)],
        compiler_params=_params(("parallel", "parallel")),
        name="attn",
    )(q3, kv3, kv3)
    return o.reshape(batch * seq, qd)


def _out_proj_kernel(o_ref, w_ref, h_hbm, out_hbm, h_buf, out_buf, h_sem, out_sem, *, seq):
    tm = o_ref.shape[0]
    i = pl.program_id(0)
    last = pl.num_programs(0) - 1
    slot = _prefetch_residue_tile(h_hbm, h_buf, h_sem, tm=tm, seq=seq)
    write = functools.partial(_residue_tile_copies, out_hbm, out_buf, out_sem, tm=tm, seq=seq, to_hbm=True)

    @pl.when(i >= 2)
    def _():
        for c in write(i - 2, slot):
            c.wait()

    out_buf[slot] = h_buf[slot] + jnp.dot(o_ref[...], w_ref[...], preferred_element_type=F32)
    for c in write(i, slot):
        c.start()

    @pl.when(i == last)
    def _():
        for c in write(i - 1, 1 - slot) + write(i, slot):
            c.wait()


def _out_proj(o, w, h, *, seq, tm=512):
    D = h.shape[-1]
    T, qd = o.shape
    assert T // tm >= 2
    row = lambda i: (i, 0)
    return pl.pallas_call(
        functools.partial(_out_proj_kernel, seq=seq),
        grid=(T // tm,),
        in_specs=[pl.BlockSpec((tm, qd), row), pl.BlockSpec((qd, D), lambda i: (0, 0)),
                  pl.BlockSpec(memory_space=pl.ANY)],
        out_specs=pl.BlockSpec(memory_space=pl.ANY),
        out_shape=jax.ShapeDtypeStruct(h.shape, F32),
        scratch_shapes=[pltpu.VMEM((2, tm, D), F32), pltpu.VMEM((2, tm, D), F32),
                        pltpu.SemaphoreType.DMA((2,)), pltpu.SemaphoreType.DMA((2,))],
        compiler_params=_params(("arbitrary",)),
        name="out_proj",
    )(o, w, h)


def _rope_tables(pos):
    seq = pos.shape[0]
    inv = ROPE_THETA ** (-jnp.arange(0, ROT_DIM, 2, dtype=F32) / ROT_DIM)
    ang = pos[:, None] * inv[None, :]
    cos, sin = jnp.cos(ang), jnp.sin(ang)
    cos_t = jnp.concatenate([cos, cos, jnp.ones((seq, HEAD_DIM - ROT_DIM), F32)], axis=1)
    sin_t = jnp.concatenate([-sin, sin, jnp.zeros((seq, HEAD_DIM - ROT_DIM), F32)], axis=1)
    return cos_t, sin_t


def kernel(x, norm_mix, norm_mlp, conv_w_in, conv_b_in, conv_w_dw, conv_b_dw, conv_ln_g, conv_ln_b,
           conv_w_out, conv_b_out, kv_norm, w_kv, attn_w_q, attn_w_o, mlp_w_in, mlp_w_out, final_norm):
    B, S, D = x.shape
    T = B * S
    assert D == D_MODEL
    row = lambda v: v.reshape(1, -1)
    rows = jnp.arange(S)
    pos = MAX_DIL * (rows % (S // MAX_DIL)) + rows // (S // MAX_DIL)
    cos_t, sin_t = _rope_tables(pos.astype(F32))

    h = x.reshape(T, D)

    h = _conv_mixer(h, row(norm_mix[0]), conv_w_in[0].astype(BF16), row(conv_b_in[0]), conv_w_dw[0],
                    row(conv_b_dw[0]), row(conv_ln_g[0]), row(conv_ln_b[0]),
                    conv_w_out[0].astype(BF16), row(conv_b_out[0]), batch=B, seq=S)
    h = _mlp(h, row(norm_mlp[0]), mlp_w_in, mlp_w_out, row(final_norm), layer=0, final_norm=False)

    h4 = h.reshape(B, S // MAX_DIL, MAX_DIL, D)

    q, kv = _qkv_proj(h4, row(norm_mix[1]), row(kv_norm), attn_w_q[0].astype(BF16), w_kv.astype(BF16),
                      cos_t, sin_t, seq=S, q_scale=LOG2_E / math.sqrt(HEAD_DIM))

    o = _attention(q, kv, batch=B, seq=S)
    h = _out_proj(o, attn_w_o[0].astype(BF16), h4, seq=S).reshape(T, D)
    h = _mlp(h, row(norm_mlp[1]), mlp_w_in, mlp_w_out, row(final_norm), layer=1, final_norm=True)
    return h.reshape(B, S, D)
```

```python
import functools
import math

import jax
import jax.numpy as jnp
from jax import lax
from jax.experimental import pallas as pl
from jax.experimental.pallas import tpu as pltpu

F32 = jnp.float32
BF16 = jnp.bfloat16

D_MODEL = 2048
CONV_WIDTH = 31
HEAD_DIM = 128
N_HEADS = 16
N_KV_HEADS = 4
GROUP = N_HEADS // N_KV_HEADS
KV_DIM = N_KV_HEADS * HEAD_DIM
WINDOWS = (128, 512, 2048)
DILATIONS = (1, 4, 16)
BLOCK = 128
ROPE_THETA = 500000.0
ROT_DIM = HEAD_DIM // 4
D_FF = 4 * D_MODEL
NORM_EPS = 1e-6
LN_EPS = 1e-5

LANES = 128
SUBLANES = 8
HALO = 32
VMEM_LIMIT = 56 * 1024 * 1024
MASK_VALUE = -1e30
LOG2_E = math.log2(math.e)
MAX_DIL = max(DILATIONS)
ATTN_UNROLL = 16


def _params(sem):
    return pltpu.CompilerParams(dimension_semantics=sem, vmem_limit_bytes=VMEM_LIMIT)


def _rmsnorm_bf16(x, g):
    ms = jnp.mean(x * x, axis=-1, keepdims=True)
    return (x * lax.rsqrt(ms + NORM_EPS) * g).astype(BF16)


def _conv_mixer_kernel(x_ref, g_ref, win_ref, bin_ref, wdw_ref, bdw_ref, lng_ref, lnb_ref, wo_ref, bo_ref,
                       o_ref, ext_ref, cv_ref, *, ts, tn):
    i = pl.program_id(1)

    @pl.when(i == 0)
    def _():
        ext_ref[0:HALO, :] = jnp.zeros((HALO, D_MODEL), F32)

    @pl.when(i > 0)
    def _():
        ext_ref[0:HALO, :] = ext_ref[ts:ts + HALO, :]

    x = x_ref[...]
    xn = _rmsnorm_bf16(x, g_ref[...])

    for c in range(D_MODEL // tn):
        cs = slice(c * tn, (c + 1) * tn)
        gs = slice(D_MODEL + c * tn, D_MODEL + (c + 1) * tn)
        a = jnp.dot(xn, win_ref[:, cs], preferred_element_type=F32) + bin_ref[:, cs]
        gate = jnp.dot(xn, win_ref[:, gs], preferred_element_type=F32) + bin_ref[:, gs]
        ext_ref[HALO:HALO + ts, cs] = a * jax.nn.sigmoid(gate)

        for lt in range(c * tn // LANES, (c + 1) * tn // LANES):
            ls = slice(lt * LANES, (lt + 1) * LANES)
            acc = None
            for rho in range(SUBLANES):
                z = None
                for d in range(rho, CONV_WIDTH, SUBLANES):
                    start = HALO - SUBLANES - (d - rho)
                    k = CONV_WIDTH - 1 - d
                    term = ext_ref[start:start + ts + SUBLANES, ls] * wdw_ref[k:k + 1, ls]
                    z = term if z is None else z + term
                part = z[SUBLANES - rho:SUBLANES - rho + ts]
                acc = part if acc is None else acc + part
            cv_ref[:, ls] = acc + bdw_ref[:, ls]

    cv = cv_ref[...]
    mu = jnp.mean(cv, axis=-1, keepdims=True)
    xc = cv - mu
    var = jnp.mean(xc * xc, axis=-1, keepdims=True)
    y = xc * lax.rsqrt(var + LN_EPS) * lng_ref[...] + lnb_ref[...]
    y = (y * jax.nn.sigmoid(y)).astype(BF16)
    o_ref[...] = x + bo_ref[...] + jnp.dot(y, wo_ref[...], preferred_element_type=F32)


def _conv_mixer(h, g, w_in, b_in, w_dw, b_dw, ln_g, ln_b, w_out, b_out, *, batch, seq, ts=512, tn=256):
    D = D_MODEL
    n = seq // ts
    row = lambda b, i: (b * n + i, 0)
    const = lambda b, i: (0, 0)
    resident = lambda shape: pl.BlockSpec(shape, const, pipeline_mode=pl.Buffered(1))
    return pl.pallas_call(
        functools.partial(_conv_mixer_kernel, ts=ts, tn=tn),
        grid=(batch, n),
        in_specs=[
            pl.BlockSpec((ts, D), row),
            resident((1, D)),
            resident((D, 2 * D)),
            resident((1, 2 * D)),
            resident((CONV_WIDTH, D)),
            resident((1, D)),
            resident((1, D)),
            resident((1, D)),
            resident((D, D)),
            resident((1, D)),
        ],
        out_specs=pl.BlockSpec((ts, D), row),
        out_shape=jax.ShapeDtypeStruct((batch * seq, D), F32),
        scratch_shapes=[pltpu.VMEM((HALO + ts, D), F32), pltpu.VMEM((ts, D), F32)],
        compiler_params=_params(("parallel", "arbitrary")),
        name="conv_mixer",
    )(h, g, w_in, b_in, w_dw, b_dw, ln_g, ln_b, w_out, b_out)


def _mlp_kernel(h_ref, g_ref, w1_ref, w2_ref, gf_ref, o_ref, xn_ref, *, final_norm):
    f = pl.program_id(1)

    @pl.when(f == 0)
    def _():
        h = h_ref[...]
        xn_ref[...] = _rmsnorm_bf16(h, g_ref[...])
        o_ref[...] = h

    w1 = w1_ref[...].astype(BF16)
    w2 = w2_ref[...].astype(BF16)
    a = jnp.maximum(jnp.dot(xn_ref[...], w1, preferred_element_type=F32), 0.0)
    o_ref[...] += jnp.dot((a * a).astype(BF16), w2, preferred_element_type=F32)

    if final_norm:
        @pl.when(f == pl.num_programs(1) - 1)
        def _():
            h = o_ref[...]
            ms = jnp.mean(h * h, axis=-1, keepdims=True)
            o_ref[...] = h * lax.rsqrt(ms + NORM_EPS) * gf_ref[...]


def _mlp(h, g, w1, w2, gf, *, layer, final_norm, tm=1024, tf=512):
    T, D = h.shape
    return pl.pallas_call(
        functools.partial(_mlp_kernel, final_norm=final_norm),
        grid=(T // tm, D_FF // tf),
        in_specs=[
            pl.BlockSpec((tm, D), lambda i, f: (i, 0)),
            pl.BlockSpec((1, D), lambda i, f: (0, 0)),
            pl.BlockSpec((None, D, tf), lambda i, f: (layer, 0, f)),
            pl.BlockSpec((None, tf, D), lambda i, f: (layer, f, 0)),
            pl.BlockSpec((1, D), lambda i, f: (0, 0)),
        ],
        out_specs=pl.BlockSpec((tm, D), lambda i, f: (i, 0)),
        out_shape=jax.ShapeDtypeStruct((T, D), F32),
        scratch_shapes=[pltpu.VMEM((tm, D), BF16)],
        compiler_params=_params(("parallel", "arbitrary")),
        name="mlp",
    )(h, g, w1, w2, gf)


def _rope_tile(x, cos_t, sin_t):
    lane = lax.broadcasted_iota(jnp.int32, (x.shape[0], HEAD_DIM), 1)
    first_half = lane < ROT_DIM // 2
    outs = []
    for hh in range(x.shape[1] // HEAD_DIM):
        xh = x[:, hh * HEAD_DIM:(hh + 1) * HEAD_DIM]
        partner = jnp.where(first_half,
                            pltpu.roll(xh, HEAD_DIM - ROT_DIM // 2, 1),
                            pltpu.roll(xh, ROT_DIM // 2, 1))
        outs.append(xh * cos_t + partner * sin_t)
    return jnp.concatenate(outs, axis=1)


def _residue_tile_copies(hbm_ref, buf_ref, sem_ref, tile, slot, *, tm, seq, to_hbm):
    rpc = seq // MAX_DIL
    tiles_per_seq = seq // tm
    b = tile // tiles_per_seq
    c0 = (tile % tiles_per_seq) * (tm // rpc)
    copies = []
    for k in range(tm // rpc):
        hbm_rows = hbm_ref.at[b, :, c0 + k, :]
        vmem_rows = buf_ref.at[slot, pl.ds(k * rpc, rpc), :]
        src, dst = (vmem_rows, hbm_rows) if to_hbm else (hbm_rows, vmem_rows)
        copies.append(pltpu.make_async_copy(src, dst, sem_ref.at[slot]))
    return copies


def _prefetch_residue_tile(hbm_ref, buf_ref, sem_ref, *, tm, seq):
    i = pl.program_id(0)
    slot = i % 2
    fetch = functools.partial(_residue_tile_copies, hbm_ref, buf_ref, sem_ref, tm=tm, seq=seq, to_hbm=False)

    @pl.when(i == 0)
    def _():
        for c in fetch(0, 0):
            c.start()

    @pl.when(i + 1 < pl.num_programs(0))
    def _():
        for c in fetch(i + 1, 1 - slot):
            c.start()

    for c in fetch(i, slot):
        c.wait()
    return slot


def _qkv_kernel(x_hbm, gq_ref, gkv_ref, wq_ref, wkv_ref, cos_ref, sin_ref, q_ref, kv_ref, x_buf, x_sem,
                *, q_scale, sub, tn, seq):
    tm = q_ref.shape[0]
    slot = _prefetch_residue_tile(x_hbm, x_buf, x_sem, tm=tm, seq=seq)
    for r in range(tm // sub):
        rs = slice(r * sub, (r + 1) * sub)
        x = x_buf[slot, rs, :]
        xs = x * lax.rsqrt(jnp.mean(x * x, axis=-1, keepdims=True) + NORM_EPS)
        xq = (xs * gq_ref[...]).astype(BF16)
        xkv = (xs * gkv_ref[...]).astype(BF16)
        cos_t, sin_t = cos_ref[rs, :], sin_ref[rs, :]
        for c in range(q_ref.shape[1] // tn):
            cs = slice(c * tn, (c + 1) * tn)
            acc = jnp.dot(xq, wq_ref[:, cs].astype(BF16), preferred_element_type=F32)
            q_ref[rs, cs] = _rope_tile(acc, cos_t, sin_t) * q_scale
        k = jnp.dot(xkv, wkv_ref[:, :KV_DIM].astype(BF16), preferred_element_type=F32)
        kv_ref[rs, :KV_DIM] = _rope_tile(k, cos_t, sin_t)
        kv_ref[rs, KV_DIM:] = jnp.dot(xkv, wkv_ref[:, KV_DIM:].astype(BF16), preferred_element_type=F32)


def _qkv_proj(x, g_q, g_kv, w_q, w_kv, cos_t, sin_t, *, seq, q_scale, tm=512, sub=256, tn=512):
    D = x.shape[-1]
    T = x.shape[0] * seq
    qd, kvd = w_q.shape[1], w_kv.shape[1]
    spb = seq // tm
    row = lambda i: (i, 0)
    const = lambda i: (0, 0)
    table = pl.BlockSpec((tm, HEAD_DIM), lambda i: (i % spb, 0))
    return pl.pallas_call(
        functools.partial(_qkv_kernel, q_scale=q_scale, sub=sub, tn=tn, seq=seq),
        grid=(T // tm,),
        in_specs=[
            pl.BlockSpec(memory_space=pl.ANY),
            pl.BlockSpec((1, D), const),
            pl.BlockSpec((1, D), const),
            pl.BlockSpec((D, qd), const, pipeline_mode=pl.Buffered(1)),
            pl.BlockSpec((D, kvd), const, pipeline_mode=pl.Buffered(1)),
            table, table,
        ],
        out_specs=[pl.BlockSpec((tm, qd), row), pl.BlockSpec((tm, kvd), row)],
        out_shape=[jax.ShapeDtypeStruct((T, qd), F32), jax.ShapeDtypeStruct((T, kvd), F32)],
        scratch_shapes=[pltpu.VMEM((2, tm, D), F32), pltpu.SemaphoreType.DMA((2,))],
        compiler_params=_params(("arbitrary",)),
        name="qkv_proj",
    )(x, g_q, g_kv, w_q, w_kv, cos_t, sin_t)


def _attn_kernel(q_ref, k_ref, v_ref, o_ref, acc_ref, m_ref, l_ref, eye_ref, bias_ref, *, seq):
    rpc = seq // MAX_DIL
    gq = GROUP * BLOCK

    def mask_t(nk, q_major_shift, k_major_shift, minor_scale, base):
        j = lax.broadcasted_iota(jnp.int32, (nk, BLOCK), 0)
        e = lax.broadcasted_iota(jnp.int32, (nk, BLOCK), 1)
        qmaj, qmin = e >> q_major_shift, e & ((1 << q_major_shift) - 1)
        kmaj, kmin = j >> k_major_shift, j & ((1 << k_major_shift) - 1)
        dist = base + minor_scale * (qmin - kmin) + (qmaj - kmaj)
        return jnp.where(jnp.logical_and(dist >= 0, dist <= BLOCK), 0.0, MASK_VALUE).astype(BF16)

    per1 = BLOCK // MAX_DIL
    dil = DILATIONS[1]
    sub = MAX_DIL // dil
    per4 = BLOCK // sub
    bias_ref[0] = mask_t(2 * BLOCK, 3, 4, MAX_DIL, BLOCK)
    bias_ref[1] = mask_t(2 * BLOCK, 3, 4, MAX_DIL, 0)
    bias_ref[2] = mask_t(2 * BLOCK, 5, 6, sub, BLOCK)
    bias_ref[3] = mask_t(2 * BLOCK, 5, 6, sub, 0)
    bias_ref[4, 0:BLOCK, :] = mask_t(BLOCK, 7, 7, 1, 0)
    row = lax.broadcasted_iota(jnp.int32, (gq, BLOCK), 0) & (BLOCK - 1)
    col = lax.broadcasted_iota(jnp.int32, (gq, BLOCK), 1)
    eye_ref[...] = jnp.where(row == col, 1.0, 0.0).astype(BF16)

    def gather(ref, chunks, lanes):
        return jnp.concatenate([ref[0, pl.ds(st, sz), lanes] for st, sz in chunks], axis=0)

    def block(q_chunks, kv_chunks, bias_t, first, last):
        nk = bias_t.shape[0]
        kk = gather(k_ref, kv_chunks, slice(None)).astype(BF16)
        vv = gather(v_ref, kv_chunks, slice(None)).astype(BF16)
        qh = jnp.concatenate(
            [gather(q_ref, q_chunks, slice(g * HEAD_DIM, (g + 1) * HEAD_DIM)) for g in range(GROUP)],
            axis=0).astype(BF16)
        q_aug = jnp.concatenate([qh, eye_ref[...]], axis=1)
        k_aug = jnp.concatenate([kk, bias_t], axis=1)
        v_aug = jnp.concatenate([vv, jnp.ones((nk, LANES), BF16)], axis=1)
        s = lax.dot_general(q_aug, k_aug, (((1,), (1,)), ((), ())), preferred_element_type=F32)
        m_b = jnp.broadcast_to(jnp.max(s, axis=-1, keepdims=True), (gq, LANES))

        def state(ref):
            return jnp.concatenate([gather(ref.at[g:g + 1], q_chunks, slice(None))
                                    for g in range(GROUP)], axis=0)

        if first:
            m_new = m_b
        else:
            m_old = state(m_ref)
            m_new = jnp.maximum(m_old, m_b)
        p = jnp.exp2(s - jnp.concatenate([m_new] * (nk // LANES), axis=1))
        o_aug = jnp.dot(p.astype(BF16), v_aug, preferred_element_type=F32)
        o_b, l_b = o_aug[:, :HEAD_DIM], o_aug[:, HEAD_DIM:]
        if first:
            acc, l = o_b, l_b
        else:
            alpha = jnp.exp2(m_old - m_new)
            acc = alpha * state(acc_ref) + o_b
            l = alpha * state(l_ref) + l_b

        if last:
            out = (acc / l).astype(o_ref.dtype)
            for g in range(GROUP):
                off = g * BLOCK
                for st, sz in q_chunks:
                    o_ref[0, pl.ds(st, sz), g * HEAD_DIM:(g + 1) * HEAD_DIM] = out[off:off + sz]
                    off += sz
        else:
            for g in range(GROUP):
                off = g * BLOCK
                for st, sz in q_chunks:
                    acc_ref[g, pl.ds(st, sz), :] = acc[off:off + sz]
                    m_ref[g, pl.ds(st, sz), :] = m_new[off:off + sz]
                    l_ref[g, pl.ds(st, sz), :] = l[off:off + sz]
                    off += sz

    def body1(nb, carry):
        q0 = pl.multiple_of(nb * per1, per1)
        k0 = pl.multiple_of(jnp.maximum(nb - 1, 0) * per1, per1)
        block([(c * rpc + q0, per1) for c in range(MAX_DIL)],
              [(c * rpc + k0, 2 * per1) for c in range(MAX_DIL)],
              bias_ref[jnp.where(nb > 0, 0, 1)], first=True, last=False)
        return carry

    lax.fori_loop(0, seq // BLOCK, body1, 0, unroll=ATTN_UNROLL)

    nblk4 = seq // (dil * BLOCK)

    def body4(idx, carry):
        r4, n = idx // nblk4, idx % nblk4
        q0 = pl.multiple_of(n * per4, per4)
        k0 = pl.multiple_of(jnp.maximum(n - 1, 0) * per4, per4)
        block([((r4 + dil * c) * rpc + q0, per4) for c in range(sub)],
              [((r4 + dil * c) * rpc + k0, 2 * per4) for c in range(sub)],
              bias_ref[jnp.where(n > 0, 2, 3)], first=False, last=False)
        return carry

    lax.fori_loop(0, dil * nblk4, body4, 0, unroll=ATTN_UNROLL)

    def body16(c, carry):
        r0 = pl.multiple_of(c * rpc, rpc)
        block([(r0, BLOCK)], [(r0, BLOCK)], bias_ref[4, 0:BLOCK, :], first=False, last=True)
        return carry

    lax.fori_loop(0, MAX_DIL, body16, 0, unroll=ATTN_UNROLL)


def _attention(q, kv, *, batch, seq):
    assert DILATIONS == (1, 4, 16) and all(w // d == BLOCK for w, d in zip(WINDOWS, DILATIONS))
    assert seq == MAX_DIL * BLOCK
    qd = N_HEADS * HEAD_DIM
    q3 = q.reshape(batch, seq, qd)
    kv3 = kv.reshape(batch, seq, 2 * KV_DIM)
    gw = GROUP * HEAD_DIM
    o = pl.pallas_call(
        functools.partial(_attn_kernel, seq=seq),
        grid=(batch, N_KV_HEADS),
        in_specs=[
            pl.BlockSpec((1, seq, gw), lambda b, h: (b, 0, h)),
            pl.BlockSpec((1, seq, HEAD_DIM), lambda b, h: (b, 0, h)),
            pl.BlockSpec((1, seq, HEAD_DIM), lambda b, h: (b, 0, N_KV_HEADS + h)),
        ],
        out_specs=pl.BlockSpec((1, seq, gw), lambda b, h: (b, 0, h)),
        out_shape=jax.ShapeDtypeStruct((batch, seq, qd), BF16),
        scratch_shapes=[pltpu.VMEM((GROUP, seq, LANES), F32)] * 3
        + [pltpu.VMEM((GROUP * BLOCK, BLOCK), BF16), pltpu.VMEM((5, 2 * BLOCK, BLOCK), BF16)],
        compiler_params=_params(("parallel", "parallel")),
        name="attn",
    )(q3, kv3, kv3)
    return o.reshape(batch * seq, qd)


def _out_proj_kernel(o_ref, w_ref, h_hbm, out_hbm, h_buf, out_buf, h_sem, out_sem, *, seq):
    tm = o_ref.shape[0]
    i = pl.program_id(0)
    last = pl.num_programs(0) - 1
    slot = _prefetch_residue_tile(h_hbm, h_buf, h_sem, tm=tm, seq=seq)
    write = functools.partial(_residue_tile_copies, out_hbm, out_buf, out_sem, tm=tm, seq=seq, to_hbm=True)

    @pl.when(i >= 2)
    def _():
        for c in write(i - 2, slot):
            c.wait()

    out_buf[slot] = h_buf[slot] + jnp.dot(o_ref[...], w_ref[...].astype(BF16), preferred_element_type=F32)
    for c in write(i, slot):
        c.start()

    @pl.when(i == last)
    def _():
        for c in write(i - 1, 1 - slot) + write(i, slot):
            c.wait()


def _out_proj(o, w, h, *, seq, tm=512):
    D = h.shape[-1]
    T, qd = o.shape
    assert T // tm >= 2
    row = lambda i: (i, 0)
    return pl.pallas_call(
        functools.partial(_out_proj_kernel, seq=seq),
        grid=(T // tm,),
        in_specs=[pl.BlockSpec((tm, qd), row), pl.BlockSpec((qd, D), lambda i: (0, 0), pipeline_mode=pl.Buffered(1)),
                  pl.BlockSpec(memory_space=pl.ANY)],
        out_specs=pl.BlockSpec(memory_space=pl.ANY),
        out_shape=jax.ShapeDtypeStruct(h.shape, F32),
        scratch_shapes=[pltpu.VMEM((2, tm, D), F32), pltpu.VMEM((2, tm, D), F32),
                        pltpu.SemaphoreType.DMA((2,)), pltpu.SemaphoreType.DMA((2,))],
        compiler_params=_params(("arbitrary",)),
        name="out_proj",
    )(o, w, h)


def _rope_tables(pos):
    seq = pos.shape[0]
    inv = ROPE_THETA ** (-jnp.arange(0, ROT_DIM, 2, dtype=F32) / ROT_DIM)
    ang = pos[:, None] * inv[None, :]
    cos, sin = jnp.cos(ang), jnp.sin(ang)
    cos_t = jnp.concatenate([cos, cos, jnp.ones((seq, HEAD_DIM - ROT_DIM), F32)], axis=1)
    sin_t = jnp.concatenate([-sin, sin, jnp.zeros((seq, HEAD_DIM - ROT_DIM), F32)], axis=1)
    return cos_t, sin_t


def kernel(x, norm_mix, norm_mlp, conv_w_in, conv_b_in, conv_w_dw, conv_b_dw, conv_ln_g, conv_ln_b,
           conv_w_out, conv_b_out, kv_norm, w_kv, attn_w_q, attn_w_o, mlp_w_in, mlp_w_out, final_norm):
    B, S, D = x.shape
    T = B * S
    assert D == D_MODEL
    row = lambda v: v.reshape(1, -1)
    rows = jnp.arange(S)
    pos = MAX_DIL * (rows % (S // MAX_DIL)) + rows // (S // MAX_DIL)
    cos_t, sin_t = _rope_tables(pos.astype(F32))

    h = x.reshape(T, D)

    h = _conv_mixer(h, row(norm_mix[0]), conv_w_in[0].astype(BF16), row(conv_b_in[0]), conv_w_dw[0],
                    row(conv_b_dw[0]), row(conv_ln_g[0]), row(conv_ln_b[0]),
                    conv_w_out[0].astype(BF16), row(conv_b_out[0]), batch=B, seq=S)
    h = _mlp(h, row(norm_mlp[0]), mlp_w_in, mlp_w_out, row(final_norm), layer=0, final_norm=False)

    h4 = h.reshape(B, S // MAX_DIL, MAX_DIL, D)

    q, kv = _qkv_proj(h4, row(norm_mix[1]), row(kv_norm), attn_w_q[0], w_kv,
                      cos_t, sin_t, seq=S, q_scale=LOG2_E / math.sqrt(HEAD_DIM))

    o = _attention(q, kv, batch=B, seq=S)
    h = _out_proj(o, attn_w_o[0], h4, seq=S).reshape(T, D)
    h = _mlp(h, row(norm_mlp[1]), mlp_w_in, mlp_w_out, row(final_norm), layer=1, final_norm=True)
    return h.reshape(B, S, D)
```
